```python
import math, functools
import jax, jax.numpy as jnp
from jax import lax
import numpy as np

D_MODEL = 2048
BATCH = 8
SEQ = 4096
DEPTH = 2

HEAD_DIM = 128
N_MIX_HEADS = D_MODEL // HEAD_DIM
SB_HEADS = N_MIX_HEADS // 2
GDN_HEADS = N_MIX_HEADS - SB_HEADS
MOBA_HEADS = N_MIX_HEADS // 2
DIL_HEADS = N_MIX_HEADS - MOBA_HEADS
SB_W = SB_HEADS * HEAD_DIM
GDN_W = GDN_HEADS * HEAD_DIM
MOBA_W = MOBA_HEADS * HEAD_DIM
DIL_W = DIL_HEADS * HEAD_DIM
MIX_W_EVEN = SB_W + GDN_W
MIX_W_ODD = MOBA_W + DIL_W
SB_BLOCK = 128
GDN_CHUNK = 64
GDN_CONV = 4
MOBA_BLOCK = 256
MOBA_TOPK = 3
MOBA_QCHUNK = 32
DIL_WINDOWS = (128, 512, 2048)
DIL_RATES = (1, 4, 16)
D_FF = ((8 * D_MODEL + 3 * 256 - 1) // (3 * 256)) * 256
NORM_EPS = 1e-6
N_EVEN = (DEPTH + 1) // 2
N_ODD = DEPTH // 2
EVEN_SIZES = (SB_W, SB_W, SB_W, 3 * GDN_W, GDN_W, GDN_HEADS, GDN_HEADS)
EVEN_IN = sum(EVEN_SIZES)
ODD_SIZES = (MOBA_W, MOBA_W, MOBA_W, DIL_W, DIL_W, DIL_W)
ODD_IN = sum(ODD_SIZES)

kernel_name = "hybrid_stickbreak_gdn_moba_dilated"


def _split_points(sizes):
    return list(np.cumsum(sizes)[:-1])


def rms_norm(x, w):
    xf = x.astype(jnp.float32)
    y = xf * lax.rsqrt(jnp.mean(xf * xf, axis=-1, keepdims=True) + NORM_EPS)
    return (y * w.astype(jnp.float32)).astype(x.dtype)


def l2norm(t):
    return t * lax.rsqrt(jnp.sum(t * t, axis=-1, keepdims=True) + NORM_EPS)


def to_heads(t, n):
    b, s, _ = t.shape
    return t.reshape(b, s, n, -1).transpose(0, 2, 1, 3)


def from_heads(t):
    b, h, s, d = t.shape
    return t.transpose(0, 2, 1, 3).reshape(b, s, h * d)


def causal_depthwise_conv(x, w):
    k, c = w.shape
    return lax.conv_general_dilated(
        x, w[:, None, :].astype(x.dtype), window_strides=(1,), padding=[(k - 1, 0)],
        dimension_numbers=("NWC", "WIO", "NWC"), feature_group_count=c)


def stick_breaking_attention(q, k, v):
    b, h, s, d = q.shape
    nq = s // SB_BLOCK
    scale = d ** -0.5
    qb = q.reshape(b, h, nq, SB_BLOCK, d).transpose(2, 0, 1, 3, 4)
    kpos = jnp.arange(s)

    def block(args):
        qc, i = args
        z = jnp.einsum("bhqd,bhkd->bhqk", qc, k).astype(jnp.float32) * scale
        qpos = i * SB_BLOCK + jnp.arange(SB_BLOCK)
        past = kpos[None, :] < qpos[:, None]
        log_1m = jnp.where(past, jax.nn.log_sigmoid(-z), 0.0)
        between = lax.cumsum(log_1m, axis=3, reverse=True) - log_1m
        wts = jnp.where(past, jnp.exp(jax.nn.log_sigmoid(z) + between), 0.0)
        return jnp.einsum("bhqk,bhkd->bhqd", wts.astype(v.dtype), v)

    out = lax.map(block, (qb, jnp.arange(nq)))
    return out.transpose(1, 2, 0, 3, 4).reshape(b, h, s, d)


def gated_delta_rule(q, k, v, g, beta):
    b, h, s, dk = q.shape
    dv = v.shape[-1]
    c = GDN_CHUNK
    n = s // c
    f32 = jnp.float32
    q = l2norm(q.astype(f32)) * dk ** -0.5
    k = l2norm(k.astype(f32))
    v = v.astype(f32)
    chunk = lambda t: t.reshape((b, h, n, c) + t.shape[3:])
    q, k, v, g, beta = map(chunk, (q, k, v, g.astype(f32), beta.astype(f32)))
    g = jnp.cumsum(g, axis=-1)
    incl = jnp.tril(jnp.ones((c, c), bool))
    strict = jnp.tril(jnp.ones((c, c), bool), -1)
    diff = g[..., :, None] - g[..., None, :]
    decay = jnp.where(incl, jnp.exp(jnp.where(incl, diff, 0.0)), 0.0)
    kb = k * beta[..., None]
    lower = jnp.where(strict, jnp.einsum("bhncd,bhnkd->bhnck", kb, k) * decay, 0.0)
    tmat = lower + jnp.eye(c, dtype=f32)
    solve = functools.partial(lax.linalg.triangular_solve, left_side=True, lower=True,
                              unit_diagonal=True)
    u = solve(tmat, v * beta[..., None])
    w = solve(tmat, kb * jnp.exp(g)[..., None])
    intra = jnp.where(incl, jnp.einsum("bhncd,bhnkd->bhnck", q, k) * decay, 0.0)
    qg = q * jnp.exp(g)[..., None]
    g_last = g[..., -1]
    kd = k * jnp.exp(g_last[..., None] - g)[..., None]

    def step(state, inp):
        qg_c, kd_c, u_c, w_c, intra_c, gl_c = inp
        v_new = u_c - jnp.einsum("bhcd,bhde->bhce", w_c, state)
        o = (jnp.einsum("bhcd,bhde->bhce", qg_c, state)
             + jnp.einsum("bhck,bhke->bhce", intra_c, v_new))
        state = state * jnp.exp(gl_c)[..., None, None] + jnp.einsum("bhcd,bhce->bhde", kd_c, v_new)
        return state, o

    xs = tuple(jnp.moveaxis(t, 2, 0) for t in (qg, kd, u, w, intra, g_last))
    _, o = lax.scan(step, jnp.zeros((b, h, dk, dv), f32), xs)
    return jnp.moveaxis(o, 0, 2).reshape(b, h, s, dv)


def moba_attention(q, k, v):
    b, h, s, d = q.shape
    bs = MOBA_BLOCK
    sp = -(-s // bs) * bs
    pad = ((0, 0), (0, 0), (0, sp - s), (0, 0))
    q, k, v = (jnp.pad(t, pad) for t in (q, k, v))
    nb = sp // bs
    ksel = min(MOBA_TOPK, nb - 1)
    scale = d ** -0.5
    kb = k.reshape(b, h, nb, bs, d)
    vb = v.reshape(b, h, nb, bs, d)
    own = jnp.arange(sp) // bs
    if ksel > 0:
        kmean = jnp.mean(kb.astype(jnp.float32), axis=3)
        gate = jnp.einsum("bhsd,bhnd->bhsn", q.astype(jnp.float32), kmean)
        fully_past = jnp.arange(nb)[None, :] < own[:, None]
        gate = jnp.where(fully_past, gate, -jnp.inf)
        _, sel = lax.top_k(gate, ksel)
    else:
        sel = jnp.zeros((b, h, sp, 0), jnp.int32)
    valid = sel < own[None, None, :, None]
    nq = sp // MOBA_QCHUNK
    qs = jnp.moveaxis(q.reshape(b, h, nq, MOBA_QCHUNK, d), 2, 0)
    sels = jnp.moveaxis(sel.reshape(b, h, nq, MOBA_QCHUNK, ksel), 2, 0)
    vals = jnp.moveaxis(valid.reshape(b, h, nq, MOBA_QCHUNK, ksel), 2, 0)
    bi = jnp.arange(b)[:, None, None]
    hi = jnp.arange(h)[None, :, None]

    def chunk(args):
        qc, selc, valc, ci = args
        ob = (ci * MOBA_QCHUNK) // bs
        k_own = lax.dynamic_index_in_dim(kb, ob, axis=2, keepdims=False)
        v_own = lax.dynamic_index_in_dim(vb, ob, axis=2, keepdims=False)
        qpos = ci * MOBA_QCHUNK + jnp.arange(MOBA_QCHUNK)
        kpos = ob * bs + jnp.arange(bs)
        s_own = jnp.einsum("bhqd,bhkd->bhqk", qc, k_own).astype(jnp.float32) * scale
        s_own = jnp.where(kpos[None, :] <= qpos[:, None], s_own, -jnp.inf)
        scores, v_sel = [], []
        for j in range(ksel):
            kj = kb[bi, hi, selc[..., j]]
            v_sel.append(vb[bi, hi, selc[..., j]])
            sj = jnp.einsum("bhqd,bhqkd->bhqk", qc, kj).astype(jnp.float32) * scale
            scores.append(jnp.where(valc[..., j, None], sj, -jnp.inf))
        p = jax.nn.softmax(jnp.concatenate(scores + [s_own], axis=-1), axis=-1).astype(v.dtype)
        o = jnp.einsum("bhqk,bhkd->bhqd", p[..., ksel * bs:], v_own)
        for j in range(ksel):
            o = o + jnp.einsum("bhqk,bhqkd->bhqd", p[..., j * bs:(j + 1) * bs], v_sel[j])
        return o

    out = lax.map(chunk, (qs, sels, vals, jnp.arange(nq)))
    return jnp.moveaxis(out, 0, 2).reshape(b, h, sp, d)[:, :, :s]


def dilated_branch(q, k, v, window, rate):
    b, h, s, d = q.shape
    span = window // rate
    seg = rate * span
    sp = -(-s // seg) * seg
    nblk = sp // seg
    scale = d ** -0.5

    def to_residue(t):
        t = jnp.pad(t, ((0, 0), (0, 0), (0, sp - s), (0, 0)))
        t = t.reshape(b, h, sp // rate, rate, d).transpose(0, 1, 3, 2, 4)
        return t.reshape(b, h, rate, nblk, span, d)

    qr, kr, vr = to_residue(q), to_residue(k), to_residue(v)
    prev = lambda t: jnp.pad(t, ((0, 0),) * 3 + ((1, 0), (0, 0), (0, 0)))[:, :, :, :-1]
    kk = jnp.concatenate([prev(kr), kr], axis=4)
    vv = jnp.concatenate([prev(vr), vr], axis=4)
    sc = jnp.einsum("bhrnqd,bhrnkd->bhrnqk", qr, kk).astype(jnp.float32) * scale
    qi = jnp.arange(span)[:, None]
    ki = jnp.arange(2 * span)[None, :]
    dist = span + qi - ki
    blk = jnp.arange(nblk)[:, None, None]
    ok = (dist >= 0) & (dist <= span) & ((blk > 0) | (ki >= span))[...]
    sc = jnp.where(ok, sc, -jnp.inf)
    m = jnp.max(sc, axis=-1, keepdims=True)
    p = jnp.exp(sc - m)
    den = jnp.sum(p, axis=-1)
    o = jnp.einsum("bhrnqk,bhrnkd->bhrnqd", p, vv.astype(jnp.float32)) / den[..., None]
    lse = m[..., 0] + jnp.log(den)
    o = o.reshape(b, h, rate, sp // rate, d).transpose(0, 1, 3, 2, 4).reshape(b, h, sp, d)[:, :, :s]
    lse = lse.reshape(b, h, rate, sp // rate).transpose(0, 1, 3, 2).reshape(b, h, sp)[:, :, :s]
    return o, lse


def dilated_attention(q, k, v):
    outs, lses = [], []
    for window, rate in zip(DIL_WINDOWS, DIL_RATES):
        o, lse = dilated_branch(q, k, v, window, rate)
        outs.append(o)
        lses.append(lse)
    wts = jax.nn.softmax(jnp.stack(lses, 0), axis=0)
    return jnp.einsum("gbhs,gbhsd->bhsd", wts, jnp.stack(outs, 0))


def even_mixer(u, w_in, conv_w, a_log, dt_bias, onorm_w, w_out):
    proj = u @ w_in
    sb_q, sb_k, sb_v, gdn_qkv, gdn_z, gdn_a, gdn_b = jnp.split(proj, _split_points(EVEN_SIZES), axis=-1)
    o_sb = stick_breaking_attention(to_heads(sb_q, SB_HEADS), to_heads(sb_k, SB_HEADS),
                                    to_heads(sb_v, SB_HEADS))
    qkv = jax.nn.silu(causal_depthwise_conv(gdn_qkv, conv_w))
    g_q, g_k, g_v = jnp.split(qkv, 3, axis=-1)
    f32 = jnp.float32
    beta = jax.nn.sigmoid(gdn_b.astype(f32)).transpose(0, 2, 1)
    g = (-jnp.exp(a_log.astype(f32)) * jax.nn.softplus(gdn_a.astype(f32) + dt_bias.astype(f32))
         ).transpose(0, 2, 1)
    o_gdn = gated_delta_rule(to_heads(g_q, GDN_HEADS), to_heads(g_k, GDN_HEADS),
                             to_heads(g_v, GDN_HEADS), g, beta)
    o_gdn = rms_norm(o_gdn, onorm_w) * jax.nn.silu(to_heads(gdn_z, GDN_HEADS).astype(f32))
    mixed = jnp.concatenate([from_heads(o_sb), from_heads(o_gdn).astype(u.dtype)], axis=-1)
    return mixed @ w_out


def odd_mixer(u, w_in, w_out):
    proj = u @ w_in
    mq, mk, mv, dq, dk, dv = jnp.split(proj, _split_points(ODD_SIZES), axis=-1)
    o_moba = moba_attention(to_heads(mq, MOBA_HEADS), to_heads(mk, MOBA_HEADS), to_heads(mv, MOBA_HEADS))
    o_dil = dilated_attention(to_heads(dq, DIL_HEADS), to_heads(dk, DIL_HEADS), to_heads(dv, DIL_HEADS))
    mixed = jnp.concatenate([from_heads(o_moba), from_heads(o_dil).astype(u.dtype)], axis=-1)
    return mixed @ w_out


def swiglu(u, w_gate, w_up, w_down):
    return (jax.nn.silu(u @ w_gate) * (u @ w_up)) @ w_down


def setup_inputs(seed: int = 0) -> dict:
    key = jax.random.key(seed)
    ks = jax.random.split(key, 20)
    f32 = jnp.float32
    nrm = lambda k, shape, sc: jax.random.normal(k, shape, f32) * sc
    gain = lambda k, shape: 1.0 + 0.02 * jax.random.normal(k, shape, f32)
    x = nrm(ks[0], (BATCH, SEQ, D_MODEL), 1.0)
    dt = jnp.exp(jax.random.uniform(ks[8], (N_EVEN, GDN_HEADS), f32, math.log(1e-3), math.log(1e-1)))
    return {
        "x": x,
        "mix_norm_pre": gain(ks[1], (DEPTH, D_MODEL)),
        "mix_norm_post": gain(ks[2], (DEPTH, D_MODEL)),
        "ffn_norm_pre": gain(ks[3], (DEPTH, D_MODEL)),
        "ffn_norm_post": gain(ks[4], (DEPTH, D_MODEL)),
        "ev_w_in": nrm(ks[5], (N_EVEN, D_MODEL, EVEN_IN), D_MODEL ** -0.5),
        "ev_conv_w": nrm(ks[6], (N_EVEN, GDN_CONV, 3 * GDN_W), GDN_CONV ** -0.5),
        "ev_a_log": jnp.log(jax.random.uniform(ks[7], (N_EVEN, GDN_HEADS), f32, 1.0, 16.0)),
        "ev_dt_bias": dt + jnp.log(-jnp.expm1(-dt)),
        "ev_onorm": gain(ks[9], (N_EVEN, HEAD_DIM)),
        "ev_w_out": nrm(ks[10], (N_EVEN, MIX_W_EVEN, D_MODEL), MIX_W_EVEN ** -0.5),
        "od_w_in": nrm(ks[11], (N_ODD, D_MODEL, ODD_IN), D_MODEL ** -0.5),
        "od_w_out": nrm(ks[12], (N_ODD, MIX_W_ODD, D_MODEL), MIX_W_ODD ** -0.5),
        "ffn_w_gate": nrm(ks[13], (DEPTH, D_MODEL, D_FF), D_MODEL ** -0.5),
        "ffn_w_up": nrm(ks[14], (DEPTH, D_MODEL, D_FF), D_MODEL ** -0.5),
        "ffn_w_down": nrm(ks[15], (DEPTH, D_FF, D_MODEL), D_FF ** -0.5),
    }


def reference(x, mix_norm_pre, mix_norm_post, ffn_norm_pre, ffn_norm_post, ev_w_in, ev_conv_w,
              ev_a_log, ev_dt_bias, ev_onorm, ev_w_out, od_w_in, od_w_out, ffn_w_gate, ffn_w_up,
              ffn_w_down):
    h = x
    for layer in range(DEPTH):
        i = layer // 2
        u = rms_norm(h, mix_norm_pre[layer])
        if layer % 2 == 0:
            u = even_mixer(u, ev_w_in[i], ev_conv_w[i], ev_a_log[i], ev_dt_bias[i], ev_onorm[i], ev_w_out[i])
        else:
            u = odd_mixer(u, od_w_in[i], od_w_out[i])
        h = h + rms_norm(u, mix_norm_post[layer])
        u = swiglu(rms_norm(h, ffn_norm_pre[layer]), ffn_w_gate[layer], ffn_w_up[layer], ffn_w_down[layer])
        h = h + rms_norm(u, ffn_norm_post[layer])
    return h
```

```python
import functools

import jax
import jax.numpy as jnp
from jax import lax
from jax.experimental import pallas as pl
from jax.experimental.pallas import tpu as pltpu

F32 = jnp.float32
BF16 = jnp.bfloat16

HEAD_DIM = 128
N_HEADS_HALF = 8
GROUP_W = N_HEADS_HALF * HEAD_DIM
NORM_EPS = 1e-6
GDN_CHUNK = 64
GDN_CONV = 4
MOBA_BLOCK = 256
MOBA_TOPK = 3
DIL_SPAN = 128
DIL_RATES = (1, 4, 16)

V7X_LANES = 128
V7X_VMEM_LIMIT_BYTES = 56 * 1024 * 1024

NEG_INF = float("-inf")


def _cparams(*sem):
    return pltpu.CompilerParams(dimension_semantics=sem, vmem_limit_bytes=V7X_VMEM_LIMIT_BYTES)


def _dot(a, b):
    return jnp.dot(a, b, preferred_element_type=F32)


def _dot_nt(a, b):
    return lax.dot_general(a, b, (((1,), (1,)), ((), ())), preferred_element_type=F32)


def _dot_tn(a, b):
    return lax.dot_general(a, b, (((0,), (0,)), ((), ())), preferred_element_type=F32)


def _split2(x):
    hi = x.astype(BF16)
    lo = (x - hi.astype(F32)).astype(BF16)
    return hi, lo


def _split3(x):
    p1 = x.astype(BF16)
    r1 = x - p1.astype(F32)
    p2 = r1.astype(BF16)
    p3 = (r1 - p2.astype(F32)).astype(BF16)
    return p1, p2, p3


def _dot3(a, b):
    a1, a2 = _split2(a)
    b1, b2 = _split2(b)
    return _dot(a1, b1) + _dot(a1, b2) + _dot(a2, b1)


def _dot3_nt(a, b):
    a1, a2 = _split2(a)
    b1, b2 = _split2(b)
    return _dot_nt(a1, b1) + _dot_nt(a1, b2) + _dot_nt(a2, b1)


def _softplus(x):
    return jnp.maximum(x, 0.0) + jnp.log1p(jnp.exp(-jnp.abs(x)))


def _sigmoid(x):
    return 1.0 / (1.0 + jnp.exp(-x))


def _rms(x, gain):
    return x * lax.rsqrt(jnp.mean(x * x, axis=-1, keepdims=True) + NORM_EPS) * gain


def _prenorm_kernel(x_ref, g_ref, o_ref):
    o_ref[...] = _rms(x_ref[...], g_ref[...]).astype(o_ref.dtype)


def _prenorm(x2d, gain, tm=512):
    m, d = x2d.shape
    return pl.pallas_call(
        _prenorm_kernel,
        out_shape=jax.ShapeDtypeStruct((m, d), BF16),
        grid=(m // tm,),
        in_specs=[pl.BlockSpec((tm, d), lambda i: (i, 0)),
                  pl.BlockSpec((1, d), lambda i: (0, 0))],
        out_specs=pl.BlockSpec((tm, d), lambda i: (i, 0)),
        compiler_params=_cparams("parallel"),
        name="prenorm",
    )(x2d, gain.reshape(1, d))


def _matmul_kernel(x_ref, w_ref, o_ref):
    o_ref[...] = _dot(x_ref[...], w_ref[...]).astype(o_ref.dtype)


def _matmul(x, w, out_dtype=F32, tm=1024, tn=512):
    m, k = x.shape
    n = w.shape[1]
    tn = min(tn, n)
    return pl.pallas_call(
        _matmul_kernel,
        out_shape=jax.ShapeDtypeStruct((m, n), out_dtype),
        grid=(m // tm, n // tn),
        in_specs=[pl.BlockSpec((tm, k), lambda i, j: (i, 0)),
                  pl.BlockSpec((k, tn), lambda i, j: (0, j))],
        out_specs=pl.BlockSpec((tm, tn), lambda i, j: (i, j)),
        compiler_params=_cparams("parallel", "arbitrary"),
        name="proj_matmul",
    )(x, w)


SB_TQ = 256
SB_TK = 256


def _sb_kernel(q_ref, k_ref, v_ref, o_ref, kb_ref, vb_ref, acc_ref, carry_ref):
    i = pl.program_id(2)

    @pl.when(i == 0)
    def _():
        kb_ref[...] = k_ref[0].astype(BF16)
        vb_ref[...] = v_ref[0].astype(BF16)

    scale = HEAD_DIM ** -0.5
    q = q_ref[0].astype(BF16)
    acc_ref[...] = jnp.zeros_like(acc_ref)
    carry_ref[...] = jnp.zeros_like(carry_ref)

    row = lax.broadcasted_iota(jnp.int32, (SB_TQ, V7X_LANES), 0)
    lane = lax.broadcasted_iota(jnp.int32, (SB_TQ, V7X_LANES), 1)
    qpos = i * SB_TQ + row
    kk = lax.broadcasted_iota(jnp.int32, (V7X_LANES, 2 * V7X_LANES), 0)
    cc = lax.broadcasted_iota(jnp.int32, (V7X_LANES, 2 * V7X_LANES), 1)
    suffix_op = jnp.where((cc >= V7X_LANES) | (kk > cc), 1.0, 0.0).astype(BF16)

    def body(t, c):
        j = i - t
        for sub in range(SB_TK // V7X_LANES - 1, -1, -1):
            start = pl.multiple_of(j * SB_TK + sub * V7X_LANES, V7X_LANES)
            ks = kb_ref[pl.ds(start, V7X_LANES), :]
            vs = vb_ref[pl.ds(start, V7X_LANES), :]
            z = _dot_nt(q, ks) * scale
            sp = _softplus(z)
            past = (start + lane) < qpos
            log_1m = jnp.where(past, -sp, 0.0)
            hi, lo = _split2(log_1m)
            sr = _dot(hi, suffix_op) + _dot(lo, suffix_op)
            carry = carry_ref[...]
            logw = (z - sp) + sr[:, :V7X_LANES] + carry
            w = jnp.where(past, jnp.exp(logw), 0.0)
            acc_ref[...] += _dot(w.astype(BF16), vs)
            carry_ref[...] = carry + sr[:, V7X_LANES:]
        return c

    lax.fori_loop(0, i + 1, body, 0)
    o_ref[0] = acc_ref[...].astype(o_ref.dtype)


def _sb_attention(proj, q_col, k_col, v_col):
    b, s, _ = proj.shape
    h = N_HEADS_HALF
    return pl.pallas_call(
        _sb_kernel,
        out_shape=jax.ShapeDtypeStruct((b, s, GROUP_W), BF16),
        grid=(b, h, s // SB_TQ),
        in_specs=[pl.BlockSpec((1, SB_TQ, HEAD_DIM), lambda bi, hi, i: (bi, i, q_col + hi)),
                  pl.BlockSpec((1, s, HEAD_DIM), lambda bi, hi, i: (bi, 0, k_col + hi)),
                  pl.BlockSpec((1, s, HEAD_DIM), lambda bi, hi, i: (bi, 0, v_col + hi))],
        out_specs=pl.BlockSpec((1, SB_TQ, HEAD_DIM), lambda bi, hi, i: (bi, i, hi)),
        scratch_shapes=[pltpu.VMEM((s, HEAD_DIM), BF16), pltpu.VMEM((s, HEAD_DIM), BF16),
                        pltpu.VMEM((SB_TQ, HEAD_DIM), F32), pltpu.VMEM((SB_TQ, HEAD_DIM), F32)],
        compiler_params=_cparams("parallel", "parallel", "arbitrary"),
        name="stickbreak_attn",
    )(proj, proj, proj)


GDN_ROWS = 512
GDN_GROUP = 256
GDN_HALO = 8


def _gdn_kernel(xq_ref, xk_ref, xv_ref, z_ref, ab_ref, wq_ref, wk_ref, wv_ref, alog_ref, dtb_ref,
                onorm_ref, o_ref, xbuf_ref, state_ref, u_ref, w_ref, vnew_ref, ointer_ref):
    h = pl.program_id(1)
    s = pl.program_id(2)
    rows = GDN_ROWS
    c = GDN_CHUNK

    @pl.when(s == 0)
    def _():
        state_ref[...] = jnp.zeros_like(state_ref)
        xbuf_ref[:, 0:GDN_HALO, :] = jnp.zeros((3, GDN_HALO, HEAD_DIM), F32)

    conv = []
    for idx, (x_ref, cw_ref) in enumerate(((xq_ref, wq_ref), (xk_ref, wk_ref), (xv_ref, wv_ref))):
        xbuf_ref[idx, GDN_HALO:GDN_HALO + rows, :] = x_ref[0]
        y = jnp.zeros((rows, HEAD_DIM), F32)
        for tap in range(GDN_CONV):
            shift = GDN_CONV - 1 - tap
            y = y + cw_ref[tap:tap + 1, :] * xbuf_ref[idx, GDN_HALO - shift:GDN_HALO - shift + rows, :]
        xbuf_ref[idx, 0:GDN_HALO, :] = xbuf_ref[idx, rows:rows + GDN_HALO, :]
        conv.append(y * _sigmoid(y))
    yq, yk, yv = conv
    q = yq * lax.rsqrt(jnp.sum(yq * yq, axis=-1, keepdims=True) + NORM_EPS) * (HEAD_DIM ** -0.5)
    k = yk * lax.rsqrt(jnp.sum(yk * yk, axis=-1, keepdims=True) + NORM_EPS)
    v = yv

    ab = ab_ref[0]
    g_all = -jnp.exp(alog_ref[...]) * _softplus(ab + dtb_ref[...])
    beta_all = _sigmoid(ab)
    er = lax.broadcasted_iota(jnp.int32, (V7X_LANES, V7X_LANES), 0)
    sel_g = jnp.where(er == h, 1.0, 0.0).astype(BF16)
    sel_b = jnp.where(er == h + N_HEADS_HALF, 1.0, 0.0).astype(BF16)
    ri = lax.broadcasted_iota(jnp.int32, (rows, rows), 0)
    ci = lax.broadcasted_iota(jnp.int32, (rows, rows), 1)
    cum_op = jnp.where((ri // c == ci // c) & (ci <= ri), 1.0, 0.0).astype(BF16)
    gc = jnp.zeros((rows, HEAD_DIM), F32)
    for piece in _split3(g_all):
        gc = gc + _dot(cum_op, _dot(piece, sel_g).astype(BF16))
    beta = jnp.zeros((rows, HEAD_DIM), F32)
    for piece in _split3(beta_all):
        beta = beta + _dot(piece, sel_b)

    gc3 = gc.reshape(rows // c, c, HEAD_DIM)
    gl = jnp.broadcast_to(gc3[:, c - 1:c, :], (rows // c, c, HEAD_DIM)).reshape(rows, HEAD_DIM)
    eg = jnp.exp(gc)
    egl = jnp.exp(gl)
    kd = k * jnp.exp(gl - gc)
    qg = q * eg
    kb = k * beta
    rhs_uw = jnp.concatenate([v * beta, kb * eg], axis=1)

    lane = lax.broadcasted_iota(jnp.int32, (GDN_GROUP, V7X_LANES), 1)
    gi = lax.broadcasted_iota(jnp.int32, (GDN_GROUP, GDN_GROUP), 0)
    gj = lax.broadcasted_iota(jnp.int32, (GDN_GROUP, GDN_GROUP), 1)
    same = (gi // c) == (gj // c)
    lower_incl = same & (gj <= gi)
    lower_strict = same & (gj < gi)
    eye = jnp.where(gi == gj, 1.0, 0.0)

    intras = []
    for grp in range(rows // GDN_GROUP):
        r0 = grp * GDN_GROUP
        sl = slice(r0, r0 + GDN_GROUP)
        p1, p2, p3 = (piece.astype(F32) for piece in _split3(gc[sl]))
        lhs = jnp.where(lane == 0, p1, jnp.where(lane == 1, p2, jnp.where(lane == 2, p3,
              jnp.where(lane < 6, 1.0, 0.0))))
        rhs = jnp.where(lane < 3, 1.0, jnp.where(lane == 3, -p1, jnp.where(lane == 4, -p2,
              jnp.where(lane == 5, -p3, 0.0))))
        diff = _dot_nt(lhs.astype(BF16), rhs.astype(BF16))
        decay = jnp.exp(jnp.where(lower_incl, diff, 0.0))
        kbb = kb[sl].astype(BF16)
        kgb = k[sl].astype(BF16)
        a_mat = jnp.where(lower_strict, _dot_nt(kbb, kgb) * decay, 0.0)
        intras.append(jnp.where(lower_incl, _dot_nt(q[sl].astype(BF16), kgb) * decay, 0.0))
        p = -a_mat
        x = eye + p
        for _ in range(5):
            p = _dot3(p, p)
            x = x + _dot3(x, p)
        uw = _dot3(x, rhs_uw[sl])
        u_ref[sl, :] = uw[:, :HEAD_DIM]
        w_ref[sl, :] = uw[:, HEAD_DIM:]

    for ch in range(rows // c):
        sl = slice(ch * c, (ch + 1) * c)
        st = state_ref[...]
        stb = st.astype(BF16)
        v_new = u_ref[sl, :] - _dot(w_ref[sl, :].astype(BF16), stb)
        vnew_ref[sl, :] = v_new
        ointer_ref[sl, :] = _dot(qg[sl].astype(BF16), stb)
        state_ref[...] = st * egl[ch * c:ch * c + 1, :] + _dot_tn(kd[sl].astype(BF16), v_new.astype(BF16))

    zz = z_ref[0]
    gate = zz * _sigmoid(zz)
    for grp in range(rows // GDN_GROUP):
        sl = slice(grp * GDN_GROUP, (grp + 1) * GDN_GROUP)
        o = ointer_ref[sl, :] + _dot(intras[grp].astype(BF16), vnew_ref[sl, :].astype(BF16))
        o_ref[0, sl, :] = (_rms(o, onorm_ref[...]) * gate[sl]).astype(o_ref.dtype)


def _gdn(proj, ab, conv_w, alog_row, dtb_row, onorm_row):
    b, s, _ = proj.shape
    h = N_HEADS_HALF
    row = lambda col: pl.BlockSpec((1, GDN_ROWS, HEAD_DIM), lambda bi, hi, si: (bi, si, col + hi))
    cw = lambda col: pl.BlockSpec((GDN_CONV, HEAD_DIM), lambda bi, hi, si: (0, col + hi))
    vec = pl.BlockSpec((1, V7X_LANES), lambda bi, hi, si: (0, 0))
    return pl.pallas_call(
        _gdn_kernel,
        out_shape=jax.ShapeDtypeStruct((b, s, GROUP_W), BF16),
        grid=(b, h, s // GDN_ROWS),
        in_specs=[row(0), row(h), row(2 * h), row(3 * h),
                  pl.BlockSpec((1, GDN_ROWS, V7X_LANES), lambda bi, hi, si: (bi, si, 0)),
                  cw(0), cw(h), cw(2 * h), vec, vec, vec],
        out_specs=pl.BlockSpec((1, GDN_ROWS, HEAD_DIM), lambda bi, hi, si: (bi, si, hi)),
        scratch_shapes=[pltpu.VMEM((3, GDN_ROWS + GDN_HALO, HEAD_DIM), F32),
                        pltpu.VMEM((HEAD_DIM, HEAD_DIM), F32),
                        pltpu.VMEM((GDN_ROWS, HEAD_DIM), F32), pltpu.VMEM((GDN_ROWS, HEAD_DIM), F32),
                        pltpu.VMEM((GDN_ROWS, HEAD_DIM), F32), pltpu.VMEM((GDN_ROWS, HEAD_DIM), F32)],
        compiler_params=_cparams("parallel", "parallel", "arbitrary"),
        name="gated_delta",
    )(proj, proj, proj, proj, ab, conv_w, conv_w, conv_w, alog_row, dtb_row, onorm_row)


def _moba_kernel(q_ref, k_ref, v_ref, o_ref, kb_ref, vb_ref, kmean_ref, m_ref, l_ref, acc_ref, *, seq):
    i = pl.program_id(2)
    bs = MOBA_BLOCK
    nb = seq // bs
    scale = HEAD_DIM ** -0.5

    @pl.when(i == 0)
    def _():
        kf = k_ref[0]
        kb_ref[...] = kf.astype(BF16)
        vb_ref[...] = v_ref[0].astype(BF16)
        rr = lax.broadcasted_iota(jnp.int32, (V7X_LANES, seq), 0)
        ss = lax.broadcasted_iota(jnp.int32, (V7X_LANES, seq), 1)
        onehot = jnp.where(ss // bs == rr, 1.0, 0.0).astype(BF16)
        tot = jnp.zeros((V7X_LANES, HEAD_DIM), F32)
        for piece in _split3(kf):
            tot = tot + _dot(onehot, piece)
        kmean_ref[...] = tot * (1.0 / bs)

    qf = q_ref[0]
    qb = qf.astype(BF16)
    lane = lax.broadcasted_iota(jnp.int32, (bs, V7X_LANES), 1)

    gate = jnp.where((lane < i) & (lane < nb), _dot3_nt(qf, kmean_ref[...]), NEG_INF)
    picked = jnp.zeros((bs, V7X_LANES), F32)
    for _ in range(MOBA_TOPK):
        mx = jnp.max(gate, axis=-1, keepdims=True)
        cand = jnp.where((gate == mx) & (mx > NEG_INF), lane, V7X_LANES)
        first = jnp.min(cand, axis=-1, keepdims=True)
        hit = lane == first
        picked = jnp.where(hit, 1.0, picked)
        gate = jnp.where(hit, NEG_INF, gate)
    picked_b = picked.astype(BF16)

    own0 = pl.multiple_of(i * bs, bs)
    row2 = lax.broadcasted_iota(jnp.int32, (bs, bs), 0)
    col2 = lax.broadcasted_iota(jnp.int32, (bs, bs), 1)
    s_own = jnp.where(col2 <= row2, _dot_nt(qb, kb_ref[pl.ds(own0, bs), :]) * scale, NEG_INF)
    m0 = jnp.max(s_own, axis=-1, keepdims=True)
    p0 = jnp.exp(s_own - m0)
    m_ref[...] = jnp.broadcast_to(m0, (bs, V7X_LANES))
    l_ref[...] = jnp.broadcast_to(jnp.sum(p0, axis=-1, keepdims=True), (bs, V7X_LANES))
    acc_ref[...] = _dot(p0.astype(BF16), vb_ref[pl.ds(own0, bs), :])

    er = lax.broadcasted_iota(jnp.int32, (V7X_LANES, bs), 0)

    def body(j, c):
        start = pl.multiple_of(j * bs, bs)
        rowsel = _dot(picked_b, jnp.where(er == j, 1.0, 0.0).astype(BF16))
        sc = jnp.where(rowsel > 0.5, _dot_nt(qb, kb_ref[pl.ds(start, bs), :]) * scale, NEG_INF)
        m_old = m_ref[...]
        m_new = jnp.maximum(m_old, jnp.max(sc, axis=-1, keepdims=True))
        alpha = jnp.exp(m_old - m_new)
        p = jnp.exp(sc - m_new[:, 0:1])
        l_ref[...] = alpha * l_ref[...] + jnp.sum(p, axis=-1, keepdims=True)
        acc_ref[...] = alpha * acc_ref[...] + _dot(p.astype(BF16), vb_ref[pl.ds(start, bs), :])
        m_ref[...] = m_new
        return c

    lax.fori_loop(0, i, body, 0)
    o_ref[0] = (acc_ref[...] / l_ref[...]).astype(o_ref.dtype)


def _moba(proj, q_col, k_col, v_col):
    b, s, _ = proj.shape
    h = N_HEADS_HALF
    bs = MOBA_BLOCK
    return pl.pallas_call(
        functools.partial(_moba_kernel, seq=s),
        out_shape=jax.ShapeDtypeStruct((b, s, GROUP_W), BF16),
        grid=(b, h, s // bs),
        in_specs=[pl.BlockSpec((1, bs, HEAD_DIM), lambda bi, hi, i: (bi, i, q_col + hi)),
                  pl.BlockSpec((1, s, HEAD_DIM), lambda bi, hi, i: (bi, 0, k_col + hi)),
                  pl.BlockSpec((1, s, HEAD_DIM), lambda bi, hi, i: (bi, 0, v_col + hi))],
        out_specs=pl.BlockSpec((1, bs, HEAD_DIM), lambda bi, hi, i: (bi, i, hi)),
        scratch_shapes=[pltpu.VMEM((s, HEAD_DIM), BF16), pltpu.VMEM((s, HEAD_DIM), BF16),
                        pltpu.VMEM((V7X_LANES, HEAD_DIM), F32),
                        pltpu.VMEM((bs, V7X_LANES), F32), pltpu.VMEM((bs, V7X_LANES), F32),
                        pltpu.VMEM((bs, HEAD_DIM), F32)],
        compiler_params=_cparams("parallel", "parallel", "arbitrary"),
        name="moba_attn",
    )(proj, proj, proj)


DIL_COPY_ROWS = 256


def _dil_kernel(q_ref, k_ref, v_ref, o_ref, qd_ref, kd_ref, vd_ref, obuf_ref, lbuf_ref, *, seq):
    span = DIL_SPAN
    scale = HEAD_DIM ** -0.5
    kd_ref[0:span, :] = jnp.zeros((span, HEAD_DIM), BF16)
    vd_ref[0:span, :] = jnp.zeros((span, HEAD_DIM), BF16)
    qi = lax.broadcasted_iota(jnp.int32, (span, 2 * span), 0)
    ki = lax.broadcasted_iota(jnp.int32, (span, 2 * span), 1)
    in_window = (ki >= qi) & (ki <= qi + span)

    for g, rate in enumerate(DIL_RATES):
        n = seq // rate
        ncopy = n // DIL_COPY_ROWS
        nblk = n // span

        def residue(rho, c, g=g, rate=rate, ncopy=ncopy, nblk=nblk):
            def copy(t, cc):
                src = rho + t * (DIL_COPY_ROWS * rate)
                dst = pl.multiple_of(t * DIL_COPY_ROWS, DIL_COPY_ROWS)
                if rate == 1:
                    idx = pl.ds(src, DIL_COPY_ROWS)
                else:
                    idx = pl.ds(src, DIL_COPY_ROWS, stride=rate)
                qd_ref[pl.ds(dst, DIL_COPY_ROWS), :] = q_ref[0, idx, :].astype(BF16)
                kd_ref[pl.ds(span + dst, DIL_COPY_ROWS), :] = k_ref[0, idx, :].astype(BF16)
                vd_ref[pl.ds(span + dst, DIL_COPY_ROWS), :] = v_ref[0, idx, :].astype(BF16)
                return cc

            lax.fori_loop(0, ncopy, copy, 0)

            def block(nbi, cc):
                r0 = pl.multiple_of(nbi * span, span)
                qq = qd_ref[pl.ds(r0, span), :]
                kk = kd_ref[pl.ds(r0, 2 * span), :]
                vv = vd_ref[pl.ds(r0, 2 * span), :]
                ok = in_window & ((nbi > 0) | (ki >= span))
                sc = jnp.where(ok, _dot_nt(qq, kk) * scale, NEG_INF)
                m = jnp.max(sc, axis=-1, keepdims=True)
                p = jnp.exp(sc - m)
                den = jnp.sum(p, axis=-1, keepdims=True)
                o = _dot(p.astype(BF16), vv) / den
                lse = jnp.broadcast_to(m + jnp.log(den), (span, HEAD_DIM))
                dst = rho + nbi * (span * rate)
                if rate == 1:
                    idx = pl.ds(dst, span)
                else:
                    idx = pl.ds(dst, span, stride=rate)
                obuf_ref[g, idx, :] = o
                lbuf_ref[g, idx, :] = lse
                return cc

            lax.fori_loop(0, nblk, block, 0)
            return c

        lax.fori_loop(0, rate, residue, 0)

    def merge(t, c):
        r0 = pl.multiple_of(t * DIL_COPY_ROWS, DIL_COPY_ROWS)
        sl = pl.ds(r0, DIL_COPY_ROWS)
        ls = [lbuf_ref[g, sl, :] for g in range(len(DIL_RATES))]
        mx = functools.reduce(jnp.maximum, ls)
        es = [jnp.exp(l - mx) for l in ls]
        num = functools.reduce(lambda a, b_: a + b_, [e * obuf_ref[g, sl, :] for g, e in enumerate(es)])
        den = functools.reduce(lambda a, b_: a + b_, es)
        o_ref[0, sl, :] = (num / den).astype(o_ref.dtype)
        return c

    lax.fori_loop(0, seq // DIL_COPY_ROWS, merge, 0)


def _dilated(proj, q_col, k_col, v_col):
    b, s, _ = proj.shape
    h = N_HEADS_HALF
    ng = len(DIL_RATES)
    full = lambda col: pl.BlockSpec((1, s, HEAD_DIM), lambda bi, hi: (bi, 0, col + hi))
    return pl.pallas_call(
        functools.partial(_dil_kernel, seq=s),
        out_shape=jax.ShapeDtypeStruct((b, s, GROUP_W), BF16),
        grid=(b, h),
        in_specs=[full(q_col), full(k_col), full(v_col)],
        out_specs=pl.BlockSpec((1, s, HEAD_DIM), lambda bi, hi: (bi, 0, hi)),
        scratch_shapes=[pltpu.VMEM((s, HEAD_DIM), BF16),
                        pltpu.VMEM((s + DIL_SPAN, HEAD_DIM), BF16),
                        pltpu.VMEM((s + DIL_SPAN, HEAD_DIM), BF16),
                        pltpu.VMEM((ng, s, HEAD_DIM), F32), pltpu.VMEM((ng, s, HEAD_DIM), F32)],
        compiler_params=_cparams("parallel", "parallel"),
        name="dilated_attn",
    )(proj, proj, proj)


def _residual_epilogue(y, h_ref, gpost_ref, gnext_ref, h_out_ref, u_out_ref):
    hn = h_ref[...] + _rms(y, gpost_ref[...])
    h_out_ref[...] = hn
    if u_out_ref is not None:
        u_out_ref[...] = _rms(hn, gnext_ref[...]).astype(u_out_ref.dtype)


def _outproj_kernel(a_ref, b_ref, wa_ref, wb_ref, h_ref, gpost_ref, gnext_ref, h_out_ref, u_out_ref):
    y = _dot(a_ref[...], wa_ref[...]) + _dot(b_ref[...], wb_ref[...])
    _residual_epilogue(y, h_ref, gpost_ref, gnext_ref, h_out_ref, u_out_ref)


def _outproj(a, b, wa, wb, h, g_post, g_next, tm=512):
    m, d = h.shape
    ka = a.shape[1]
    kb = b.shape[1]
    rowblk = lambda w: pl.BlockSpec((tm, w), lambda i: (i, 0))
    const = lambda r, c: pl.BlockSpec((r, c), lambda i: (0, 0))
    return pl.pallas_call(
        _outproj_kernel,
        out_shape=(jax.ShapeDtypeStruct((m, d), F32), jax.ShapeDtypeStruct((m, d), BF16)),
        grid=(m // tm,),
        in_specs=[rowblk(ka), rowblk(kb), const(ka, d), const(kb, d), rowblk(d), const(1, d), const(1, d)],
        out_specs=(rowblk(d), rowblk(d)),
        compiler_params=_cparams("parallel"),
        name="outproj_residual",
    )(a, b, wa, wb, h, g_post.reshape(1, d), g_next.reshape(1, d))


def _ffn_kernel(u_ref, wg_ref, wu_ref, wd_ref, h_ref, gpost_ref, gnext_ref, *rest, emit_next):
    if emit_next:
        h_out_ref, u_out_ref, acc_ref = rest
    else:
        h_out_ref, acc_ref = rest
        u_out_ref = None
    f = pl.program_id(1)

    @pl.when(f == 0)
    def _():
        acc_ref[...] = jnp.zeros_like(acc_ref)

    u = u_ref[...]
    gate = _dot(u, wg_ref[...])
    up = _dot(u, wu_ref[...])
    act = (gate * _sigmoid(gate) * up).astype(BF16)
    acc_ref[...] += _dot(act, wd_ref[...])

    @pl.when(f == pl.num_programs(1) - 1)
    def _():
        _residual_epilogue(acc_ref[...], h_ref, gpost_ref, gnext_ref, h_out_ref, u_out_ref)


def _ffn(u, wg, wu, wd, h, g_post, g_next, emit_next, tm=512, tf=512):
    m, d = h.shape
    ff = wg.shape[1]
    rowblk = pl.BlockSpec((tm, d), lambda i, f: (i, 0))
    const = pl.BlockSpec((1, d), lambda i, f: (0, 0))
    out_shape = [jax.ShapeDtypeStruct((m, d), F32)]
    out_specs = [rowblk]
    if emit_next:
        out_shape.append(jax.ShapeDtypeStruct((m, d), BF16))
        out_specs.append(rowblk)
    res = pl.pallas_call(
        functools.partial(_ffn_kernel, emit_next=emit_next),
        out_shape=tuple(out_shape),
        grid=(m // tm, ff // tf),
        in_specs=[rowblk,
                  pl.BlockSpec((d, tf), lambda i, f: (0, f)),
                  pl.BlockSpec((d, tf), lambda i, f: (0, f)),
                  pl.BlockSpec((tf, d), lambda i, f: (f, 0)),
                  rowblk, const, const],
        out_specs=tuple(out_specs),
        scratch_shapes=[pltpu.VMEM((tm, d), F32)],
        compiler_params=_cparams("parallel", "arbitrary"),
        name="swiglu_residual",
    )(u, wg, wu, wd, h, g_post.reshape(1, d), g_next.reshape(1, d))
    return res if emit_next else (res[0], None)


def _pad_lanes(vec, offset=0):
    out = jnp.zeros((1, V7X_LANES), F32)
    return lax.dynamic_update_slice(out, vec.reshape(1, -1).astype(F32), (0, offset))


def kernel(x, mix_norm_pre, mix_norm_post, ffn_norm_pre, ffn_norm_post, ev_w_in, ev_conv_w, ev_a_log,
           ev_dt_bias, ev_onorm, ev_w_out, od_w_in, od_w_out, ffn_w_gate, ffn_w_up, ffn_w_down):
    b, s, d = x.shape
    m = b * s
    gw = GROUP_W
    nh = N_HEADS_HALF
    h = x.reshape(m, d)

    u = _prenorm(h, mix_norm_pre[0])
    w_in = ev_w_in[0]
    n_sb = 3 * gw
    n_gdn = 4 * gw
    proj_sb = _matmul(u, w_in[:, :n_sb].astype(BF16)).reshape(b, s, n_sb)
    proj_gdn = _matmul(u, w_in[:, n_sb:n_sb + n_gdn].astype(BF16)).reshape(b, s, n_gdn)
    w_ab = jnp.pad(w_in[:, n_sb + n_gdn:], ((0, 0), (0, V7X_LANES - 2 * nh))).astype(BF16)
    proj_ab = _matmul(u, w_ab).reshape(b, s, V7X_LANES)
    o_sb = _sb_attention(proj_sb, 0, nh, 2 * nh)
    o_gdn = _gdn(proj_gdn, proj_ab, ev_conv_w[0], _pad_lanes(ev_a_log[0]), _pad_lanes(ev_dt_bias[0]),
                 ev_onorm[0].reshape(1, HEAD_DIM))
    w_out = ev_w_out[0].astype(BF16)
    h, u = _outproj(o_sb.reshape(m, gw), o_gdn.reshape(m, gw), w_out[:gw], w_out[gw:], h,
                    mix_norm_post[0], ffn_norm_pre[0])
    h, u = _ffn(u, ffn_w_gate[0].astype(BF16), ffn_w_up[0].astype(BF16), ffn_w_down[0].astype(BF16), h,
                ffn_norm_post[0], mix_norm_pre[1], emit_next=True)

    proj_od = _matmul(u, od_w_in[0].astype(BF16)).reshape(b, s, 6 * gw)
    o_moba = _moba(proj_od, 0, nh, 2 * nh)
    o_dil = _dilated(proj_od, 3 * nh, 4 * nh, 5 * nh)
    w_out = od_w_out[0].astype(BF16)
    h, u = _outproj(o_moba.reshape(m, gw), o_dil.reshape(m, gw), w_out[:gw], w_out[gw:], h,
                    mix_norm_post[1], ffn_norm_pre[1])
    h, _ = _ffn(u, ffn_w_gate[1].astype(BF16), ffn_w_up[1].astype(BF16), ffn_w_down[1].astype(BF16), h,
                ffn_norm_post[1], ffn_norm_post[1], emit_next=False)
    return h.reshape(b, s, d)
```

```python
import functools

import jax
import jax.numpy as jnp
from jax import lax
from jax.experimental import pallas as pl
from jax.experimental.pallas import tpu as pltpu

F32 = jnp.float32
BF16 = jnp.bfloat16

HEAD_DIM = 128
N_HEADS_HALF = 8
GROUP_W = N_HEADS_HALF * HEAD_DIM
NORM_EPS = 1e-6
GDN_CHUNK = 64
GDN_CONV = 4
MOBA_BLOCK = 256
MOBA_TOPK = 3
DIL_SPAN = 128
DIL_RATES = (1, 4, 16)

V7X_LANES = 128
V7X_VMEM_LIMIT_BYTES = 56 * 1024 * 1024

NEG_INF = float("-inf")


def _cparams(*sem):
    return pltpu.CompilerParams(dimension_semantics=sem, vmem_limit_bytes=V7X_VMEM_LIMIT_BYTES)


def _dot(a, b):
    return jnp.dot(a, b, preferred_element_type=F32)


def _dot_nt(a, b):
    return lax.dot_general(a, b, (((1,), (1,)), ((), ())), preferred_element_type=F32)


def _dot_tn(a, b):
    return lax.dot_general(a, b, (((0,), (0,)), ((), ())), preferred_element_type=F32)


def _split2(x):
    hi = x.astype(BF16)
    lo = (x - hi.astype(F32)).astype(BF16)
    return hi, lo


def _split3(x):
    p1 = x.astype(BF16)
    r1 = x - p1.astype(F32)
    p2 = r1.astype(BF16)
    p3 = (r1 - p2.astype(F32)).astype(BF16)
    return p1, p2, p3


def _dot3(a, b):
    a1, a2 = _split2(a)
    b1, b2 = _split2(b)
    return _dot(a1, b1) + _dot(a1, b2) + _dot(a2, b1)


def _dot3_nt(a, b):
    a1, a2 = _split2(a)
    b1, b2 = _split2(b)
    return _dot_nt(a1, b1) + _dot_nt(a1, b2) + _dot_nt(a2, b1)


def _softplus(x):
    return jnp.maximum(x, 0.0) + jnp.log1p(jnp.exp(-jnp.abs(x)))


def _sigmoid(x):
    return 1.0 / (1.0 + jnp.exp(-x))


def _rms(x, gain):
    return x * lax.rsqrt(jnp.mean(x * x, axis=-1, keepdims=True) + NORM_EPS) * gain


def _prenorm_kernel(x_ref, g_ref, o_ref):
    o_ref[...] = _rms(x_ref[...], g_ref[...]).astype(o_ref.dtype)


def _prenorm(x2d, gain, tm=512):
    m, d = x2d.shape
    return pl.pallas_call(
        _prenorm_kernel,
        out_shape=jax.ShapeDtypeStruct((m, d), BF16),
        grid=(m // tm,),
        in_specs=[pl.BlockSpec((tm, d), lambda i: (i, 0)),
                  pl.BlockSpec((1, d), lambda i: (0, 0))],
        out_specs=pl.BlockSpec((tm, d), lambda i: (i, 0)),
        compiler_params=_cparams("parallel"),
        name="prenorm",
    )(x2d, gain.reshape(1, d))


def _matmul_kernel(x_ref, w_ref, o_ref):
    o_ref[...] = _dot(x_ref[...], w_ref[...]).astype(o_ref.dtype)


def _matmul(x, w, out_dtype=F32, tm=1024, tn=512):
    m, k = x.shape
    n = w.shape[1]
    tn = min(tn, n)
    return pl.pallas_call(
        _matmul_kernel,
        out_shape=jax.ShapeDtypeStruct((m, n), out_dtype),
        grid=(m // tm, n // tn),
        in_specs=[pl.BlockSpec((tm, k), lambda i, j: (i, 0)),
                  pl.BlockSpec((k, tn), lambda i, j: (0, j))],
        out_specs=pl.BlockSpec((tm, tn), lambda i, j: (i, j)),
        compiler_params=_cparams("parallel", "arbitrary"),
        name="proj_matmul",
    )(x, w)


SB_TQ = 512
SB_PAIR = 2 * V7X_LANES


def _sb_sweep_step(q, kb_ref, vb_ref, acc_ref, carry_ref, suffix_op, start, qpos, masked):
    scale = HEAD_DIM ** -0.5
    nl = V7X_LANES
    for pair in range(SB_TQ // SB_PAIR - 1, -1, -1):
        p0 = pl.multiple_of(start + pair * SB_PAIR, SB_PAIR)
        ks = kb_ref[pl.ds(p0, SB_PAIR), :]
        vs = vb_ref[pl.ds(p0, SB_PAIR), :]
        z = _dot_nt(q, ks) * scale
        sp = jnp.maximum(z, 0.0) + jnp.log(1.0 + jnp.exp(-jnp.abs(z)))
        if masked:
            lane = lax.broadcasted_iota(jnp.int32, (SB_TQ, SB_PAIR), 1)
            past = (p0 + lane) < qpos
            spm = jnp.where(past, sp, 0.0)
        else:
            spm = sp
        hi, lo = _split2(spm)
        sr_far = _dot(jnp.concatenate([hi[:, :nl], lo[:, :nl]], axis=1), suffix_op)
        sr_near = _dot(jnp.concatenate([hi[:, nl:], lo[:, nl:]], axis=1), suffix_op)
        c_near = carry_ref[...]
        c_far = c_near + sr_near[:, nl:]
        carry_ref[...] = c_far + sr_far[:, nl:]
        between = jnp.concatenate([sr_far[:, :nl] + c_far, sr_near[:, :nl] + c_near], axis=1)
        w = jnp.exp((z - sp) - between)
        if masked:
            w = jnp.where(past, w, 0.0)
        acc_ref[...] += _dot(w.astype(BF16), vs)


def _sb_kernel(q_ref, k_ref, v_ref, o_ref, kb_ref, vb_ref, acc_ref, carry_ref):
    i = pl.program_id(2)

    @pl.when(i == 0)
    def _():
        kb_ref[...] = k_ref[0].astype(BF16)
        vb_ref[...] = v_ref[0].astype(BF16)

    q = q_ref[0].astype(BF16)
    acc_ref[...] = jnp.zeros_like(acc_ref)
    carry_ref[...] = jnp.zeros_like(carry_ref)

    kk = lax.broadcasted_iota(jnp.int32, (SB_PAIR, SB_PAIR), 0) % V7X_LANES
    cc = lax.broadcasted_iota(jnp.int32, (SB_PAIR, SB_PAIR), 1)
    suffix_op = jnp.where((cc >= V7X_LANES) | (kk > cc), 1.0, 0.0).astype(BF16)

    qpos = i * SB_TQ + lax.broadcasted_iota(jnp.int32, (SB_TQ, SB_PAIR), 0)
    _sb_sweep_step(q, kb_ref, vb_ref, acc_ref, carry_ref, suffix_op, i * SB_TQ, qpos, masked=True)

    def body(t, c):
        _sb_sweep_step(q, kb_ref, vb_ref, acc_ref, carry_ref, suffix_op, (i - t) * SB_TQ, None, masked=False)
        return c

    lax.fori_loop(1, i + 1, body, 0)
    o_ref[0] = acc_ref[...].astype(o_ref.dtype)


def _sb_attention(proj, q_col, k_col, v_col):
    b, s, _ = proj.shape
    h = N_HEADS_HALF
    return pl.pallas_call(
        _sb_kernel,
        out_shape=jax.ShapeDtypeStruct((b, s, GROUP_W), BF16),
        grid=(b, h, s // SB_TQ),
        in_specs=[pl.BlockSpec((1, SB_TQ, HEAD_DIM), lambda bi, hi, i: (bi, i, q_col + hi)),
                  pl.BlockSpec((1, s, HEAD_DIM), lambda bi, hi, i: (bi, 0, k_col + hi)),
                  pl.BlockSpec((1, s, HEAD_DIM), lambda bi, hi, i: (bi, 0, v_col + hi))],
        out_specs=pl.BlockSpec((1, SB_TQ, HEAD_DIM), lambda bi, hi, i: (bi, i, hi)),
        scratch_shapes=[pltpu.VMEM((s, HEAD_DIM), BF16), pltpu.VMEM((s, HEAD_DIM), BF16),
                        pltpu.VMEM((SB_TQ, HEAD_DIM), F32), pltpu.VMEM((SB_TQ, HEAD_DIM), F32)],
        compiler_params=_cparams("parallel", "parallel", "arbitrary"),
        name="stickbreak_attn",
    )(proj, proj, proj)


GDN_ROWS = 512
GDN_GROUP = 256
GDN_HALO = 8


def _gdn_kernel(xq_ref, xk_ref, xv_ref, z_ref, ab_ref, wq_ref, wk_ref, wv_ref, alog_ref, dtb_ref,
                onorm_ref, o_ref, xbuf_ref, state_ref, u_ref, w_ref, vnew_ref, ointer_ref):
    h = pl.program_id(1)
    s = pl.program_id(2)
    rows = GDN_ROWS
    c = GDN_CHUNK

    @pl.when(s == 0)
    def _():
        state_ref[...] = jnp.zeros_like(state_ref)
        xbuf_ref[:, 0:GDN_HALO, :] = jnp.zeros((3, GDN_HALO, HEAD_DIM), F32)

    conv = []
    for idx, (x_ref, cw_ref) in enumerate(((xq_ref, wq_ref), (xk_ref, wk_ref), (xv_ref, wv_ref))):
        xbuf_ref[idx, GDN_HALO:GDN_HALO + rows, :] = x_ref[0]
        y = jnp.zeros((rows, HEAD_DIM), F32)
        for tap in range(GDN_CONV):
            shift = GDN_CONV - 1 - tap
            y = y + cw_ref[tap:tap + 1, :] * xbuf_ref[idx, GDN_HALO - shift:GDN_HALO - shift + rows, :]
        xbuf_ref[idx, 0:GDN_HALO, :] = xbuf_ref[idx, rows:rows + GDN_HALO, :]
        conv.append(y * _sigmoid(y))
    yq, yk, yv = conv
    q = yq * lax.rsqrt(jnp.sum(yq * yq, axis=-1, keepdims=True) + NORM_EPS) * (HEAD_DIM ** -0.5)
    k = yk * lax.rsqrt(jnp.sum(yk * yk, axis=-1, keepdims=True) + NORM_EPS)
    v = yv

    ab = ab_ref[0]
    g_all = -jnp.exp(alog_ref[...]) * _softplus(ab + dtb_ref[...])
    beta_all = _sigmoid(ab)
    er = lax.broadcasted_iota(jnp.int32, (V7X_LANES, V7X_LANES), 0)
    sel_g = jnp.where(er == h, 1.0, 0.0).astype(BF16)
    sel_b = jnp.where(er == h + N_HEADS_HALF, 1.0, 0.0).astype(BF16)
    ri = lax.broadcasted_iota(jnp.int32, (rows, rows), 0)
    ci = lax.broadcasted_iota(jnp.int32, (rows, rows), 1)
    cum_op = jnp.where((ri // c == ci // c) & (ci <= ri), 1.0, 0.0).astype(BF16)
    gc = jnp.zeros((rows, HEAD_DIM), F32)
    for piece in _split3(g_all):
        gc = gc + _dot(cum_op, _dot(piece, sel_g).astype(BF16))
    beta = jnp.zeros((rows, HEAD_DIM), F32)
    for piece in _split3(beta_all):
        beta = beta + _dot(piece, sel_b)

    gc3 = gc.reshape(rows // c, c, HEAD_DIM)
    gl = jnp.broadcast_to(gc3[:, c - 1:c, :], (rows // c, c, HEAD_DIM)).reshape(rows, HEAD_DIM)
    eg = jnp.exp(gc)
    egl = jnp.exp(gl)
    kd = k * jnp.exp(gl - gc)
    qg = q * eg
    kb = k * beta
    rhs_uw = jnp.concatenate([v * beta, kb * eg], axis=1)

    lane = lax.broadcasted_iota(jnp.int32, (GDN_GROUP, V7X_LANES), 1)
    gi = lax.broadcasted_iota(jnp.int32, (GDN_GROUP, GDN_GROUP), 0)
    gj = lax.broadcasted_iota(jnp.int32, (GDN_GROUP, GDN_GROUP), 1)
    same = (gi // c) == (gj // c)
    lower_incl = same & (gj <= gi)
    lower_strict = same & (gj < gi)
    eye = jnp.where(gi == gj, 1.0, 0.0)

    intras = []
    for grp in range(rows // GDN_GROUP):
        r0 = grp * GDN_GROUP
        sl = slice(r0, r0 + GDN_GROUP)
        p1, p2, p3 = (piece.astype(F32) for piece in _split3(gc[sl]))
        lhs = jnp.where(lane == 0, p1, jnp.where(lane == 1, p2, jnp.where(lane == 2, p3,
              jnp.where(lane < 6, 1.0, 0.0))))
        rhs = jnp.where(lane < 3, 1.0, jnp.where(lane == 3, -p1, jnp.where(lane == 4, -p2,
              jnp.where(lane == 5, -p3, 0.0))))
        diff = _dot_nt(lhs.astype(BF16), rhs.astype(BF16))
        decay = jnp.exp(jnp.where(lower_incl, diff, 0.0))
        kbb = kb[sl].astype(BF16)
        kgb = k[sl].astype(BF16)
        a_mat = jnp.where(lower_strict, _dot_nt(kbb, kgb) * decay, 0.0)
        intras.append(jnp.where(lower_incl, _dot_nt(q[sl].astype(BF16), kgb) * decay, 0.0))
        p = -a_mat
        x = eye + p
        for _ in range(5):
            pb = p.astype(BF16)
            p = _dot(pb, pb)
            x = x + _dot(x.astype(BF16), p.astype(BF16))
        uw = _dot(x.astype(BF16), rhs_uw[sl].astype(BF16))
        u_ref[sl, :] = uw[:, :HEAD_DIM]
        w_ref[sl, :] = uw[:, HEAD_DIM:]

    for ch in range(rows // c):
        sl = slice(ch * c, (ch + 1) * c)
        st = state_ref[...]
        stb = st.astype(BF16)
        v_new = u_ref[sl, :] - _dot(w_ref[sl, :].astype(BF16), stb)
        vnew_ref[sl, :] = v_new
        ointer_ref[sl, :] = _dot(qg[sl].astype(BF16), stb)
        state_ref[...] = st * egl[ch * c:ch * c + 1, :] + _dot_tn(kd[sl].astype(BF16), v_new.astype(BF16))

    zz = z_ref[0]
    gate = zz * _sigmoid(zz)
    for grp in range(rows // GDN_GROUP):
        sl = slice(grp * GDN_GROUP, (grp + 1) * GDN_GROUP)
        o = ointer_ref[sl, :] + _dot(intras[grp].astype(BF16), vnew_ref[sl, :].astype(BF16))
        o_ref[0, sl, :] = (_rms(o, onorm_ref[...]) * gate[sl]).astype(o_ref.dtype)


def _gdn(proj, ab, conv_w, alog_row, dtb_row, onorm_row):
    b, s, _ = proj.shape
    h = N_HEADS_HALF
    row = lambda col: pl.BlockSpec((1, GDN_ROWS, HEAD_DIM), lambda bi, hi, si: (bi, si, col + hi))
    cw = lambda col: pl.BlockSpec((GDN_CONV, HEAD_DIM), lambda bi, hi, si: (0, col + hi))
    vec = pl.BlockSpec((1, V7X_LANES), lambda bi, hi, si: (0, 0))
    return pl.pallas_call(
        _gdn_kernel,
        out_shape=jax.ShapeDtypeStruct((b, s, GROUP_W), BF16),
        grid=(b, h, s // GDN_ROWS),
        in_specs=[row(0), row(h), row(2 * h), row(3 * h),
                  pl.BlockSpec((1, GDN_ROWS, V7X_LANES), lambda bi, hi, si: (bi, si, 0)),
                  cw(0), cw(h), cw(2 * h), vec, vec, vec],
        out_specs=pl.BlockSpec((1, GDN_ROWS, HEAD_DIM), lambda bi, hi, si: (bi, si, hi)),
        scratch_shapes=[pltpu.VMEM((3, GDN_ROWS + GDN_HALO, HEAD_DIM), F32),
                        pltpu.VMEM((HEAD_DIM, HEAD_DIM), F32),
                        pltpu.VMEM((GDN_ROWS, HEAD_DIM), F32), pltpu.VMEM((GDN_ROWS, HEAD_DIM), F32),
                        pltpu.VMEM((GDN_ROWS, HEAD_DIM), F32), pltpu.VMEM((GDN_ROWS, HEAD_DIM), F32)],
        compiler_params=_cparams("parallel", "parallel", "arbitrary"),
        name="gated_delta",
    )(proj, proj, proj, proj, ab, conv_w, conv_w, conv_w, alog_row, dtb_row, onorm_row)


MOBA_MASK_BIG = 1e30
MOBA_SWEEP_BLOCKS = (4, 2, 1)


def _moba_kernel(q_ref, k_ref, v_ref, o_ref, kaug_ref, vb_ref, kmean_ref, m_ref, l_ref, acc_ref, *, seq):
    i = pl.program_id(2)
    bs = MOBA_BLOCK
    nb = seq // bs
    nl = V7X_LANES
    exp2_scale = (HEAD_DIM ** -0.5) * 1.4426950408889634

    @pl.when(i == 0)
    def _():
        kf = k_ref[0]
        kaug_ref[:, :HEAD_DIM] = kf.astype(BF16)
        blk = lax.broadcasted_iota(jnp.int32, (seq, nl), 0) // bs
        col = lax.broadcasted_iota(jnp.int32, (seq, nl), 1)
        kaug_ref[:, HEAD_DIM:] = jnp.where(blk == col, MOBA_MASK_BIG, 0.0).astype(BF16)
        vb_ref[...] = v_ref[0].astype(BF16)
        rr = lax.broadcasted_iota(jnp.int32, (V7X_LANES, seq), 0)
        ss = lax.broadcasted_iota(jnp.int32, (V7X_LANES, seq), 1)
        onehot = jnp.where(ss // bs == rr, 1.0, 0.0).astype(BF16)
        tot = jnp.zeros((V7X_LANES, HEAD_DIM), F32)
        for piece in _split3(kf):
            tot = tot + _dot(onehot, piece)
        kmean_ref[...] = tot * (1.0 / bs)

    qf = q_ref[0]
    qb = qf.astype(BF16)
    lane = lax.broadcasted_iota(jnp.int32, (bs, V7X_LANES), 1)

    gate = jnp.where((lane < i) & (lane < nb), _dot3_nt(qf, kmean_ref[...]), NEG_INF)
    picked = jnp.zeros((bs, V7X_LANES), F32)
    for _ in range(MOBA_TOPK):
        mx = jnp.max(gate, axis=-1, keepdims=True)
        cand = jnp.where((gate == mx) & (mx > NEG_INF), lane, V7X_LANES)
        first = jnp.min(cand, axis=-1, keepdims=True)
        hit = lane == first
        picked = jnp.where(hit, 1.0, picked)
        gate = jnp.where(hit, NEG_INF, gate)
    q_aug = jnp.concatenate([qb, (picked - 1.0).astype(BF16)], axis=1)

    def raw_scores(first, nblocks):
        start = pl.multiple_of(first * bs, bs)
        return _dot_nt(q_aug, kaug_ref[pl.ds(start, nblocks * bs), :])

    def lane_groups(x):
        return [x[:, g * nl:(g + 1) * nl] for g in range(x.shape[1] // nl)]

    def sweep_past(step):
        wide = MOBA_SWEEP_BLOCKS[0]

        def wide_body(t, c):
            step(t * wide, wide)
            return c

        lax.fori_loop(0, i // wide, wide_body, 0)
        for w in MOBA_SWEEP_BLOCKS[1:]:
            @pl.when((i // w) % 2 == 1)
            def _(w=w):
                step((i // (2 * w)) * (2 * w), w)

    own0 = pl.multiple_of(i * bs, bs)
    row2 = lax.broadcasted_iota(jnp.int32, (bs, bs), 0)
    col2 = lax.broadcasted_iota(jnp.int32, (bs, bs), 1)
    s_own = jnp.where(col2 <= row2, _dot_nt(qb, kaug_ref[pl.ds(own0, bs), :HEAD_DIM]), -MOBA_MASK_BIG)

    m_ref[...] = functools.reduce(jnp.maximum, lane_groups(s_own))

    def max_step(first, nblocks):
        m_ref[...] = functools.reduce(jnp.maximum, [m_ref[...]] + lane_groups(raw_scores(first, nblocks)))

    sweep_past(max_step)
    m_ref[...] = jnp.broadcast_to(jnp.max(m_ref[...], axis=-1, keepdims=True), (bs, nl))

    def accumulate(sc, first_row, init):
        mrow = m_ref[...]
        ps = [jnp.exp2((g - mrow) * exp2_scale) for g in lane_groups(sc)]
        psum = functools.reduce(lambda a, b_: a + b_, ps)
        pv = _dot(jnp.concatenate(ps, axis=1).astype(BF16), vb_ref[pl.ds(first_row, sc.shape[1]), :])
        if init:
            l_ref[...] = psum
            acc_ref[...] = pv
        else:
            l_ref[...] += psum
            acc_ref[...] += pv

    accumulate(s_own, own0, init=True)

    def acc_step(first, nblocks):
        accumulate(raw_scores(first, nblocks), pl.multiple_of(first * bs, bs), init=False)

    sweep_past(acc_step)
    o_ref[0] = (acc_ref[...] / jnp.sum(l_ref[...], axis=-1, keepdims=True)).astype(o_ref.dtype)


def _moba(proj, q_col, k_col, v_col):
    b, s, _ = proj.shape
    h = N_HEADS_HALF
    bs = MOBA_BLOCK
    return pl.pallas_call(
        functools.partial(_moba_kernel, seq=s),
        out_shape=jax.ShapeDtypeStruct((b, s, GROUP_W), BF16),
        grid=(b, h, s // bs),
        in_specs=[pl.BlockSpec((1, bs, HEAD_DIM), lambda bi, hi, i: (bi, i, q_col + hi)),
                  pl.BlockSpec((1, s, HEAD_DIM), lambda bi, hi, i: (bi, 0, k_col + hi)),
                  pl.BlockSpec((1, s, HEAD_DIM), lambda bi, hi, i: (bi, 0, v_col + hi))],
        out_specs=pl.BlockSpec((1, bs, HEAD_DIM), lambda bi, hi, i: (bi, i, hi)),
        scratch_shapes=[pltpu.VMEM((s, HEAD_DIM + V7X_LANES), BF16), pltpu.VMEM((s, HEAD_DIM), BF16),
                        pltpu.VMEM((V7X_LANES, HEAD_DIM), F32),
                        pltpu.VMEM((bs, V7X_LANES), F32), pltpu.VMEM((bs, V7X_LANES), F32),
                        pltpu.VMEM((bs, HEAD_DIM), F32)],
        compiler_params=_cparams("parallel", "parallel", "arbitrary"),
        name="moba_attn",
    )(proj, proj, proj)


DIL_COPY_ROWS = 256
DIL_BLOCKS_PER_STEP = 4


def _dil_kernel(q_ref, k_ref, v_ref, o_ref, qd_ref, kd_ref, vd_ref, obuf_ref, lbuf_ref, *, seq):
    span = DIL_SPAN
    scale = HEAD_DIM ** -0.5
    kd_ref[0:span, :] = jnp.zeros((span, HEAD_DIM), BF16)
    vd_ref[0:span, :] = jnp.zeros((span, HEAD_DIM), BF16)
    qi = lax.broadcasted_iota(jnp.int32, (span, 2 * span), 0)
    ki = lax.broadcasted_iota(jnp.int32, (span, 2 * span), 1)
    in_window = (ki >= qi) & (ki <= qi + span)

    for g, rate in enumerate(DIL_RATES):
        n = seq // rate
        copies_per_res = n // DIL_COPY_ROWS
        nblk = n // span

        def copy(c, cc, rate=rate, copies_per_res=copies_per_res):
            rho = c // copies_per_res
            src = rho + (c % copies_per_res) * (DIL_COPY_ROWS * rate)
            dst = pl.multiple_of(c * DIL_COPY_ROWS, DIL_COPY_ROWS)
            if rate == 1:
                idx = pl.ds(src, DIL_COPY_ROWS)
            else:
                idx = pl.ds(src, DIL_COPY_ROWS, stride=rate)
            qd_ref[pl.ds(dst, DIL_COPY_ROWS), :] = q_ref[0, idx, :].astype(BF16)
            kd_ref[pl.ds(span + dst, DIL_COPY_ROWS), :] = k_ref[0, idx, :].astype(BF16)
            vd_ref[pl.ds(span + dst, DIL_COPY_ROWS), :] = v_ref[0, idx, :].astype(BF16)
            return cc

        lax.fori_loop(0, seq // DIL_COPY_ROWS, copy, 0)

        def blocks(step, cc, g=g, rate=rate, nblk=nblk):
            for sub in range(DIL_BLOCKS_PER_STEP):
                blk = step * DIL_BLOCKS_PER_STEP + sub
                rho = blk // nblk
                nbi = blk % nblk
                r0 = pl.multiple_of(blk * span, span)
                qq = qd_ref[pl.ds(r0, span), :]
                kk = kd_ref[pl.ds(r0, 2 * span), :]
                vv = vd_ref[pl.ds(r0, 2 * span), :]
                ok = in_window & ((nbi > 0) | (ki >= span))
                sc = jnp.where(ok, _dot_nt(qq, kk) * scale, NEG_INF)
                m = jnp.max(sc, axis=-1, keepdims=True)
                p = jnp.exp(sc - m)
                den = jnp.sum(p, axis=-1, keepdims=True)
                o = _dot(p.astype(BF16), vv) / den
                lse = jnp.broadcast_to(m + jnp.log(den), (span, HEAD_DIM))
                dst = rho + nbi * (span * rate)
                if rate == 1:
                    idx = pl.ds(dst, span)
                else:
                    idx = pl.ds(dst, span, stride=rate)
                obuf_ref[g, idx, :] = o
                lbuf_ref[g, idx, :] = lse
            return cc

        lax.fori_loop(0, seq // span // DIL_BLOCKS_PER_STEP, blocks, 0)

    def merge(t, c):
        r0 = pl.multiple_of(t * DIL_COPY_ROWS, DIL_COPY_ROWS)
        sl = pl.ds(r0, DIL_COPY_ROWS)
        ls = [lbuf_ref[g, sl, :] for g in range(len(DIL_RATES))]
        mx = functools.reduce(jnp.maximum, ls)
        es = [jnp.exp(l - mx) for l in ls]
        num = functools.reduce(lambda a, b_: a + b_, [e * obuf_ref[g, sl, :] for g, e in enumerate(es)])
        den = functools.reduce(lambda a, b_: a + b_, es)
        o_ref[0, sl, :] = (num / den).astype(o_ref.dtype)
        return c

    lax.fori_loop(0, seq // DIL_COPY_ROWS, merge, 0)


def _dilated(proj, q_col, k_col, v_col):
    b, s, _ = proj.shape
    h = N_HEADS_HALF
    ng = len(DIL_RATES)
    full = lambda col: pl.BlockSpec((1, s, HEAD_DIM), lambda bi, hi: (bi, 0, col + hi))
    return pl.pallas_call(
        functools.partial(_dil_kernel, seq=s),
        out_shape=jax.ShapeDtypeStruct((b, s, GROUP_W), BF16),
        grid=(b, h),
        in_specs=[full(q_col), full(k_col), full(v_col)],
        out_specs=pl.BlockSpec((1, s, HEAD_DIM), lambda bi, hi: (bi, 0, hi)),
        scratch_shapes=[pltpu.VMEM((s, HEAD_DIM), BF16),
                        pltpu.VMEM((s + DIL_SPAN, HEAD_DIM), BF16),
                        pltpu.VMEM((s + DIL_SPAN, HEAD_DIM), BF16),
                        pltpu.VMEM((ng, s, HEAD_DIM), F32), pltpu.VMEM((ng, s, HEAD_DIM), F32)],
        compiler_params=_cparams("parallel", "parallel"),
        name="dilated_attn",
    )(proj, proj, proj)


def _residual_epilogue(y, h_ref, gpost_ref, gnext_ref, h_out_ref, u_out_ref):
    hn = h_ref[...] + _rms(y, gpost_ref[...])
    h_out_ref[...] = hn
    if u_out_ref is not None:
        u_out_ref[...] = _rms(hn, gnext_ref[...]).astype(u_out_ref.dtype)


def _outproj_kernel(a_ref, b_ref, wa_ref, wb_ref, h_ref, gpost_ref, gnext_ref, h_out_ref, u_out_ref):
    y = _dot(a_ref[...], wa_ref[...]) + _dot(b_ref[...], wb_ref[...])
    _residual_epilogue(y, h_ref, gpost_ref, gnext_ref, h_out_ref, u_out_ref)


def _outproj(a, b, wa, wb, h, g_post, g_next, tm=512):
    m, d = h.shape
    ka = a.shape[1]
    kb = b.shape[1]
    rowblk = lambda w: pl.BlockSpec((tm, w), lambda i: (i, 0))
    const = lambda r, c: pl.BlockSpec((r, c), lambda i: (0, 0))
    return pl.pallas_call(
        _outproj_kernel,
        out_shape=(jax.ShapeDtypeStruct((m, d), F32), jax.ShapeDtypeStruct((m, d), BF16)),
        grid=(m // tm,),
        in_specs=[rowblk(ka), rowblk(kb), const(ka, d), const(kb, d), rowblk(d), const(1, d), const(1, d)],
        out_specs=(rowblk(d), rowblk(d)),
        compiler_params=_cparams("parallel"),
        name="outproj_residual",
    )(a, b, wa, wb, h, g_post.reshape(1, d), g_next.reshape(1, d))


def _ffn_kernel(u_ref, wg_ref, wu_ref, wd_ref, h_ref, gpost_ref, gnext_ref, *rest, emit_next):
    if emit_next:
        h_out_ref, u_out_ref, acc_ref = rest
    else:
        h_out_ref, acc_ref = rest
        u_out_ref = None
    f = pl.program_id(1)

    @pl.when(f == 0)
    def _():
        acc_ref[...] = jnp.zeros_like(acc_ref)

    u = u_ref[...]
    gate = _dot(u, wg_ref[...])
    up = _dot(u, wu_ref[...])
    act = (gate * _sigmoid(gate) * up).astype(BF16)
    acc_ref[...] += _dot(act, wd_ref[...])

    @pl.when(f == pl.num_programs(1) - 1)
    def _():
        _residual_epilogue(acc_ref[...], h_ref, gpost_ref, gnext_ref, h_out_ref, u_out_ref)


def _ffn(u, wg, wu, wd, h, g_post, g_next, emit_next, tm=512, tf=512):
    m, d = h.shape
    ff = wg.shape[1]
    rowblk = pl.BlockSpec((tm, d), lambda i, f: (i, 0))
    const = pl.BlockSpec((1, d), lambda i, f: (0, 0))
    out_shape = [jax.ShapeDtypeStruct((m, d), F32)]
    out_specs = [rowblk]
    if emit_next:
        out_shape.append(jax.ShapeDtypeStruct((m, d), BF16))
        out_specs.append(rowblk)
    res = pl.pallas_call(
        functools.partial(_ffn_kernel, emit_next=emit_next),
        out_shape=tuple(out_shape),
        grid=(m // tm, ff // tf),
        in_specs=[rowblk,
                  pl.BlockSpec((d, tf), lambda i, f: (0, f)),
                  pl.BlockSpec((d, tf), lambda i, f: (0, f)),
                  pl.BlockSpec((tf, d), lambda i, f: (f, 0)),
                  rowblk, const, const],
        out_specs=tuple(out_specs),
        scratch_shapes=[pltpu.VMEM((tm, d), F32)],
        compiler_params=_cparams("parallel", "arbitrary"),
        name="swiglu_residual",
    )(u, wg, wu, wd, h, g_post.reshape(1, d), g_next.reshape(1, d))
    return res if emit_next else (res[0], None)


def _pad_lanes(vec, offset=0):
    out = jnp.zeros((1, V7X_LANES), F32)
    return lax.dynamic_update_slice(out, vec.reshape(1, -1).astype(F32), (0, offset))


def kernel(x, mix_norm_pre, mix_norm_post, ffn_norm_pre, ffn_norm_post, ev_w_in, ev_conv_w, ev_a_log,
           ev_dt_bias, ev_onorm, ev_w_out, od_w_in, od_w_out, ffn_w_gate, ffn_w_up, ffn_w_down):
    b, s, d = x.shape
    m = b * s
    gw = GROUP_W
    nh = N_HEADS_HALF
    h = x.reshape(m, d)

    u = _prenorm(h, mix_norm_pre[0])
    w_in = ev_w_in[0]
    n_sb = 3 * gw
    n_gdn = 4 * gw
    proj_sb = _matmul(u, w_in[:, :n_sb].astype(BF16)).reshape(b, s, n_sb)
    proj_gdn = _matmul(u, w_in[:, n_sb:n_sb + n_gdn].astype(BF16)).reshape(b, s, n_gdn)
    w_ab = jnp.pad(w_in[:, n_sb + n_gdn:], ((0, 0), (0, V7X_LANES - 2 * nh))).astype(BF16)
    proj_ab = _matmul(u, w_ab).reshape(b, s, V7X_LANES)
    o_sb = _sb_attention(proj_sb, 0, nh, 2 * nh)
    o_gdn = _gdn(proj_gdn, proj_ab, ev_conv_w[0], _pad_lanes(ev_a_log[0]), _pad_lanes(ev_dt_bias[0]),
                 ev_onorm[0].reshape(1, HEAD_DIM))
    w_out = ev_w_out[0].astype(BF16)
    h, u = _outproj(o_sb.reshape(m, gw), o_gdn.reshape(m, gw), w_out[:gw], w_out[gw:], h,
                    mix_norm_post[0], ffn_norm_pre[0])
    h, u = _ffn(u, ffn_w_gate[0].astype(BF16), ffn_w_up[0].astype(BF16), ffn_w_down[0].astype(BF16), h,
                ffn_norm_post[0], mix_norm_pre[1], emit_next=True)

    proj_od = _matmul(u, od_w_in[0].astype(BF16)).reshape(b, s, 6 * gw)
    o_moba = _moba(proj_od, 0, nh, 2 * nh)
    o_dil = _dilated(proj_od, 3 * nh, 4 * nh, 5 * nh)
    w_out = od_w_out[0].astype(BF16)
    h, u = _outproj(o_moba.reshape(m, gw), o_dil.reshape(m, gw), w_out[:gw], w_out[gw:], h,
                    mix_norm_post[1], ffn_norm_pre[1])
    h, _ = _ffn(u, ffn_w_gate[1].astype(BF16), ffn_w_up[1].astype(BF16), ffn_w_down[1].astype(BF16), h,
                ffn_norm_post[1], ffn_norm_post[1], emit_next=False)
    return h.reshape(b, s, d)
```

```python
import functools

import jax
import jax.numpy as jnp
from jax import lax
from jax.experimental import pallas as pl
from jax.experimental.pallas import tpu as pltpu

F32 = jnp.float32
BF16 = jnp.bfloat16

HEAD_DIM = 128
N_HEADS_HALF = 8
GROUP_W = N_HEADS_HALF * HEAD_DIM
NORM_EPS = 1e-6
GDN_CHUNK = 64
GDN_CONV = 4
MOBA_BLOCK = 256
MOBA_TOPK = 3
DIL_SPAN = 128
DIL_RATES = (1, 4, 16)

V7X_LANES = 128
V7X_VMEM_LIMIT_BYTES = 56 * 1024 * 1024

NEG_INF = float("-inf")


def _cparams(*sem):
    return pltpu.CompilerParams(dimension_semantics=sem, vmem_limit_bytes=V7X_VMEM_LIMIT_BYTES)


def _dot(a, b):
    return jnp.dot(a, b, preferred_element_type=F32)


def _dot_nt(a, b):
    return lax.dot_general(a, b, (((1,), (1,)), ((), ())), preferred_element_type=F32)


def _dot_tn(a, b):
    return lax.dot_general(a, b, (((0,), (0,)), ((), ())), preferred_element_type=F32)


def _split2(x):
    hi = x.astype(BF16)
    lo = (x - hi.astype(F32)).astype(BF16)
    return hi, lo


def _split3(x):
    p1 = x.astype(BF16)
    r1 = x - p1.astype(F32)
    p2 = r1.astype(BF16)
    p3 = (r1 - p2.astype(F32)).astype(BF16)
    return p1, p2, p3


def _dot3(a, b):
    a1, a2 = _split2(a)
    b1, b2 = _split2(b)
    return _dot(a1, b1) + _dot(a1, b2) + _dot(a2, b1)


def _dot3_nt(a, b):
    a1, a2 = _split2(a)
    b1, b2 = _split2(b)
    return _dot_nt(a1, b1) + _dot_nt(a1, b2) + _dot_nt(a2, b1)


def _softplus(x):
    return jnp.maximum(x, 0.0) + jnp.log1p(jnp.exp(-jnp.abs(x)))


def _sigmoid(x):
    return 1.0 / (1.0 + jnp.exp(-x))


def _rms(x, gain):
    return x * lax.rsqrt(jnp.mean(x * x, axis=-1, keepdims=True) + NORM_EPS) * gain


def _prenorm_kernel(x_ref, g_ref, o_ref):
    o_ref[...] = _rms(x_ref[...], g_ref[...]).astype(o_ref.dtype)


def _prenorm(x2d, gain, tm=512):
    m, d = x2d.shape
    return pl.pallas_call(
        _prenorm_kernel,
        out_shape=jax.ShapeDtypeStruct((m, d), BF16),
        grid=(m // tm,),
        in_specs=[pl.BlockSpec((tm, d), lambda i: (i, 0)),
                  pl.BlockSpec((1, d), lambda i: (0, 0))],
        out_specs=pl.BlockSpec((tm, d), lambda i: (i, 0)),
        compiler_params=_cparams("parallel"),
        name="prenorm",
    )(x2d, gain.reshape(1, d))


def _matmul_kernel(x_ref, w_ref, o_ref):
    o_ref[...] = _dot(x_ref[...], w_ref[...]).astype(o_ref.dtype)


def _matmul(x, w, out_dtype=F32, tm=1024, tn=512):
    m, k = x.shape
    n = w.shape[1]
    tn = min(tn, n)
    return pl.pallas_call(
        _matmul_kernel,
        out_shape=jax.ShapeDtypeStruct((m, n), out_dtype),
        grid=(m // tm, n // tn),
        in_specs=[pl.BlockSpec((tm, k), lambda i, j: (i, 0)),
                  pl.BlockSpec((k, tn), lambda i, j: (0, j))],
        out_specs=pl.BlockSpec((tm, tn), lambda i, j: (i, j)),
        compiler_params=_cparams("parallel", "arbitrary"),
        name="proj_matmul",
    )(x, w)


SB_TQ = 512
SB_PAIR = 2 * V7X_LANES


def _sb_sweep_step(q, kb_ref, vb_ref, acc_ref, carry_ref, suffix_op, start, qpos, masked):
    scale = HEAD_DIM ** -0.5
    log2e = 1.4426950408889634
    nl = V7X_LANES
    pairs = range(SB_TQ // SB_PAIR - 1, -1, -1)
    p0s = [pl.multiple_of(start + pair * SB_PAIR, SB_PAIR) for pair in pairs]
    raws = [_dot_nt(q, kb_ref[pl.ds(p0, SB_PAIR), :]) for p0 in p0s]
    log_sig, pasts, parts = [], [], []
    for p0, raw in zip(p0s, raws):
        z = raw * scale
        sp = jnp.maximum(z, 0.0) + jnp.log(1.0 + jnp.exp2(jnp.abs(raw) * (-scale * log2e)))
        log_sig.append(z - sp)
        if masked:
            past = (p0 + lax.broadcasted_iota(jnp.int32, (SB_TQ, SB_PAIR), 1)) < qpos
            sp = jnp.where(past, sp, 0.0)
            pasts.append(past)
        hi, lo = _split2(sp)
        parts.append((jnp.concatenate([hi[:, nl:], lo[:, nl:]], axis=1),
                      jnp.concatenate([hi[:, :nl], lo[:, :nl]], axis=1)))
    sums = [(_dot(near, suffix_op), _dot(far, suffix_op)) for near, far in parts]
    carry = carry_ref[...]
    betweens = []
    for sr_near, sr_far in sums:
        c_far = carry + sr_near[:, nl:]
        betweens.append(jnp.concatenate([sr_far[:, :nl] + c_far, sr_near[:, :nl] + carry], axis=1))
        carry = c_far + sr_far[:, nl:]
    carry_ref[...] = carry
    ws = []
    for idx, (ls, between) in enumerate(zip(log_sig, betweens)):
        w = jnp.exp2((ls - between) * log2e)
        if masked:
            w = jnp.where(pasts[idx], w, 0.0)
        ws.append(w.astype(BF16))
    acc_ref[...] += sum(_dot(w, vb_ref[pl.ds(p0, SB_PAIR), :]) for w, p0 in zip(ws, p0s))


def _sb_kernel(q_ref, k_ref, v_ref, o_ref, kb_ref, vb_ref, acc_ref, carry_ref):
    i = pl.program_id(2)

    @pl.when(i == 0)
    def _():
        kb_ref[...] = k_ref[0].astype(BF16)
        vb_ref[...] = v_ref[0].astype(BF16)

    q = q_ref[0].astype(BF16)
    acc_ref[...] = jnp.zeros_like(acc_ref)
    carry_ref[...] = jnp.zeros_like(carry_ref)

    kk = lax.broadcasted_iota(jnp.int32, (SB_PAIR, SB_PAIR), 0) % V7X_LANES
    cc = lax.broadcasted_iota(jnp.int32, (SB_PAIR, SB_PAIR), 1)
    suffix_op = jnp.where((cc >= V7X_LANES) | (kk > cc), 1.0, 0.0).astype(BF16)

    qpos = i * SB_TQ + lax.broadcasted_iota(jnp.int32, (SB_TQ, SB_PAIR), 0)
    _sb_sweep_step(q, kb_ref, vb_ref, acc_ref, carry_ref, suffix_op, i * SB_TQ, qpos, masked=True)

    def body(t, c):
        _sb_sweep_step(q, kb_ref, vb_ref, acc_ref, carry_ref, suffix_op, (i - t) * SB_TQ, None, masked=False)
        return c

    lax.fori_loop(1, i + 1, body, 0)
    o_ref[0] = acc_ref[...].astype(o_ref.dtype)


def _sb_attention(proj, q_col, k_col, v_col):
    b, s, _ = proj.shape
    h = N_HEADS_HALF
    return pl.pallas_call(
        _sb_kernel,
        out_shape=jax.ShapeDtypeStruct((b, s, GROUP_W), BF16),
        grid=(b, h, s // SB_TQ),
        in_specs=[pl.BlockSpec((1, SB_TQ, HEAD_DIM), lambda bi, hi, i: (bi, i, q_col + hi)),
                  pl.BlockSpec((1, s, HEAD_DIM), lambda bi, hi, i: (bi, 0, k_col + hi)),
                  pl.BlockSpec((1, s, HEAD_DIM), lambda bi, hi, i: (bi, 0, v_col + hi))],
        out_specs=pl.BlockSpec((1, SB_TQ, HEAD_DIM), lambda bi, hi, i: (bi, i, hi)),
        scratch_shapes=[pltpu.VMEM((s, HEAD_DIM), BF16), pltpu.VMEM((s, HEAD_DIM), BF16),
                        pltpu.VMEM((SB_TQ, HEAD_DIM), F32), pltpu.VMEM((SB_TQ, HEAD_DIM), F32)],
        compiler_params=_cparams("parallel", "parallel", "arbitrary"),
        name="stickbreak_attn",
    )(proj, proj, proj)


GDN_ROWS = 512
GDN_GROUP = 256
GDN_HALO = 8
GDN_HEADS_PER_STEP = 2


def _gdn_prep(hd, xq_ref, xk_ref, xv_ref, ab_ref, wq_ref, wk_ref, wv_ref, alog_ref, dtb_ref, xbuf_ref):
    h = pl.program_id(1) * GDN_HEADS_PER_STEP + hd
    hl = slice(hd * HEAD_DIM, (hd + 1) * HEAD_DIM)
    rows = GDN_ROWS
    c = GDN_CHUNK

    conv = []
    for idx, (x_ref, cw_ref) in enumerate(((xq_ref, wq_ref), (xk_ref, wk_ref), (xv_ref, wv_ref))):
        xbuf_ref[hd, idx, GDN_HALO:GDN_HALO + rows, :] = x_ref[0, :, hl]
        y = jnp.zeros((rows, HEAD_DIM), F32)
        for tap in range(GDN_CONV):
            shift = GDN_CONV - 1 - tap
            y = y + cw_ref[tap:tap + 1, hl] * xbuf_ref[hd, idx, GDN_HALO - shift:GDN_HALO - shift + rows, :]
        xbuf_ref[hd, idx, 0:GDN_HALO, :] = xbuf_ref[hd, idx, rows:rows + GDN_HALO, :]
        conv.append(y * _sigmoid(y))
    yq, yk, yv = conv
    q = yq * lax.rsqrt(jnp.sum(yq * yq, axis=-1, keepdims=True) + NORM_EPS) * (HEAD_DIM ** -0.5)
    k = yk * lax.rsqrt(jnp.sum(yk * yk, axis=-1, keepdims=True) + NORM_EPS)
    v = yv

    ab = ab_ref[0]
    g_all = -jnp.exp(alog_ref[...]) * _softplus(ab + dtb_ref[...])
    beta_all = _sigmoid(ab)
    ab_lane = lax.broadcasted_iota(jnp.int32, (rows, V7X_LANES), 1)
    g_col = jnp.sum(jnp.where(ab_lane == h, g_all, 0.0), axis=-1, keepdims=True)
    beta = jnp.sum(jnp.where(ab_lane == h + N_HEADS_HALF, beta_all, 0.0), axis=-1, keepdims=True)
    g_rep = jnp.broadcast_to(g_col, (rows, HEAD_DIM))
    ri = lax.broadcasted_iota(jnp.int32, (GDN_GROUP, GDN_GROUP), 0)
    ci = lax.broadcasted_iota(jnp.int32, (GDN_GROUP, GDN_GROUP), 1)
    cum_op = jnp.where((ri // c == ci // c) & (ci <= ri), 1.0, 0.0).astype(BF16)
    gc = jnp.concatenate(
        [sum(_dot(cum_op, piece) for piece in _split3(g_rep[r0:r0 + GDN_GROUP]))
         for r0 in range(0, rows, GDN_GROUP)], axis=0)

    gc3 = gc.reshape(rows // c, c, HEAD_DIM)
    gl = jnp.broadcast_to(gc3[:, c - 1:c, :], (rows // c, c, HEAD_DIM)).reshape(rows, HEAD_DIM)
    eg = jnp.exp(gc)
    egl = jnp.exp(gl)
    kd = k * jnp.exp(gl - gc)
    qg = q * eg
    kb = k * beta
    rhs_uw = jnp.concatenate([v * beta, kb * eg], axis=1)
    return dict(q=q, k=k, kb=kb, kd=kd, qg=qg, gc=gc, egl=egl, rhs_uw=rhs_uw)


def _gdn_kernel(xq_ref, xk_ref, xv_ref, z_ref, ab_ref, wq_ref, wk_ref, wv_ref, alog_ref, dtb_ref,
                onorm_ref, o_ref, xbuf_ref, state_ref, u_ref, w_ref, vnew_ref, ointer_ref):
    rows = GDN_ROWS
    c = GDN_CHUNK
    nch = rows // c
    heads = range(GDN_HEADS_PER_STEP)
    groups = [(hd, slice(r0, r0 + GDN_GROUP)) for hd in heads for r0 in range(0, rows, GDN_GROUP)]

    @pl.when(pl.program_id(2) == 0)
    def _():
        state_ref[...] = jnp.zeros_like(state_ref)
        xbuf_ref[:, :, 0:GDN_HALO, :] = jnp.zeros((GDN_HEADS_PER_STEP, 3, GDN_HALO, HEAD_DIM), F32)

    hv = [_gdn_prep(hd, xq_ref, xk_ref, xv_ref, ab_ref, wq_ref, wk_ref, wv_ref, alog_ref, dtb_ref, xbuf_ref)
          for hd in heads]

    lane = lax.broadcasted_iota(jnp.int32, (GDN_GROUP, V7X_LANES), 1)
    gi = lax.broadcasted_iota(jnp.int32, (GDN_GROUP, GDN_GROUP), 0)
    gj = lax.broadcasted_iota(jnp.int32, (GDN_GROUP, GDN_GROUP), 1)
    same = (gi // c) == (gj // c)
    lower_incl = same & (gj <= gi)
    lower_strict = same & (gj < gi)
    eye = jnp.where(gi == gj, 1.0, 0.0)

    ps, xs, intras = [], [], []
    for hd, sl in groups:
        p1, p2, p3 = (piece.astype(F32) for piece in _split3(hv[hd]["gc"][sl]))
        lhs = jnp.where(lane == 0, p1, jnp.where(lane == 1, p2, jnp.where(lane == 2, p3,
              jnp.where(lane < 6, 1.0, 0.0))))
        rhs = jnp.where(lane < 3, 1.0, jnp.where(lane == 3, -p1, jnp.where(lane == 4, -p2,
              jnp.where(lane == 5, -p3, 0.0))))
        diff = _dot_nt(lhs.astype(BF16), rhs.astype(BF16))
        decay = jnp.exp(jnp.where(lower_incl, diff, 0.0))
        kgb = hv[hd]["k"][sl].astype(BF16)
        a_mat = jnp.where(lower_strict, _dot_nt(hv[hd]["kb"][sl].astype(BF16), kgb) * decay, 0.0)
        intras.append(jnp.where(lower_incl, _dot_nt(hv[hd]["q"][sl].astype(BF16), kgb) * decay, 0.0).astype(BF16))
        ps.append(-a_mat)
        xs.append(eye - a_mat)

    for _ in range(5):
        pbs = [p.astype(BF16) for p in ps]
        ps = [_dot(pb, pb) for pb in pbs]
        xs = [x + _dot(x.astype(BF16), p.astype(BF16)) for x, p in zip(xs, ps)]
    for (hd, sl), x in zip(groups, xs):
        uw = _dot(x.astype(BF16), hv[hd]["rhs_uw"][sl].astype(BF16))
        u_ref[hd, sl, :] = uw[:, :HEAD_DIM]
        w_ref[hd, sl, :] = uw[:, HEAD_DIM:]

    chunk = lambda ch: slice(ch * c, (ch + 1) * c)
    trans = [[None] * nch for _ in heads]
    drive = [[None] * nch for _ in heads]
    for ch in range(nch):
        for hd in heads:
            kdb = hv[hd]["kd"][chunk(ch)].astype(BF16)
            trans[hd][ch] = _dot_tn(kdb, w_ref[hd, chunk(ch), :].astype(BF16)).astype(BF16)
            drive[hd][ch] = _dot_tn(kdb, u_ref[hd, chunk(ch), :].astype(BF16))
    st = [state_ref[hd] for hd in heads]
    states = [[None] * nch for _ in heads]
    for ch in range(nch):
        for hd in heads:
            stb = st[hd].astype(BF16)
            states[hd][ch] = stb
            st[hd] = st[hd] * hv[hd]["egl"][ch * c:ch * c + 1, :] - _dot(trans[hd][ch], stb) + drive[hd][ch]
    for hd in heads:
        state_ref[hd] = st[hd]
    for ch in range(nch):
        for hd in heads:
            lhs = jnp.concatenate([w_ref[hd, chunk(ch), :], hv[hd]["qg"][chunk(ch)]], axis=0).astype(BF16)
            ws = _dot(lhs, states[hd][ch])
            vnew_ref[hd, chunk(ch), :] = u_ref[hd, chunk(ch), :] - ws[:c]
            ointer_ref[hd, chunk(ch), :] = ws[c:]

    for (hd, sl), intra in zip(groups, intras):
        hl = slice(hd * HEAD_DIM, (hd + 1) * HEAD_DIM)
        zz = z_ref[0, sl, hl]
        o = ointer_ref[hd, sl, :] + _dot(intra, vnew_ref[hd, sl, :].astype(BF16))
        o_ref[0, sl, hl] = (_rms(o, onorm_ref[...]) * (zz * _sigmoid(zz))).astype(o_ref.dtype)


def _gdn(proj, ab, conv_w, alog_row, dtb_row, onorm_row):
    b, s, _ = proj.shape
    nhd = GDN_HEADS_PER_STEP
    wblk = nhd * HEAD_DIM
    hsteps = N_HEADS_HALF // nhd
    row = lambda col: pl.BlockSpec((1, GDN_ROWS, wblk), lambda bi, hi, si: (bi, si, col + hi))
    cw = lambda col: pl.BlockSpec((GDN_CONV, wblk), lambda bi, hi, si: (0, col + hi))
    vec = pl.BlockSpec((1, V7X_LANES), lambda bi, hi, si: (0, 0))
    per_head = lambda *shape: pltpu.VMEM((nhd,) + shape, F32)
    return pl.pallas_call(
        _gdn_kernel,
        out_shape=jax.ShapeDtypeStruct((b, s, GROUP_W), BF16),
        grid=(b, hsteps, s // GDN_ROWS),
        in_specs=[row(0), row(hsteps), row(2 * hsteps), row(3 * hsteps),
                  pl.BlockSpec((1, GDN_ROWS, V7X_LANES), lambda bi, hi, si: (bi, si, 0)),
                  cw(0), cw(hsteps), cw(2 * hsteps), vec, vec, vec],
        out_specs=pl.BlockSpec((1, GDN_ROWS, wblk), lambda bi, hi, si: (bi, si, hi)),
        scratch_shapes=[per_head(3, GDN_ROWS + GDN_HALO, HEAD_DIM),
                        per_head(HEAD_DIM, HEAD_DIM),
                        per_head(GDN_ROWS, HEAD_DIM), per_head(GDN_ROWS, HEAD_DIM),
                        per_head(GDN_ROWS, HEAD_DIM), per_head(GDN_ROWS, HEAD_DIM)],
        compiler_params=_cparams("parallel", "parallel", "arbitrary"),
        name="gated_delta",
    )(proj, proj, proj, proj, ab, conv_w, conv_w, conv_w, alog_row, dtb_row, onorm_row)


MOBA_MASK_BIG = 1e30
MOBA_TQ = 2 * MOBA_BLOCK
MOBA_SWEEP_BLOCKS = (4, 2)


def _moba_kernel(q_ref, k_ref, v_ref, o_ref, kaug_ref, vb_ref, kmean_ref, m_ref, l_ref, acc_ref, *, seq):
    bs = MOBA_BLOCK
    tq = MOBA_TQ
    nb = seq // bs
    nl = V7X_LANES
    ncand = -(-nb // 8) * 8
    i = (tq // bs) * pl.program_id(2)
    exp2_scale = (HEAD_DIM ** -0.5) * 1.4426950408889634

    @pl.when(i == 0)
    def _():
        kf = k_ref[0]
        kaug_ref[:, :HEAD_DIM] = kf.astype(BF16)
        blk = lax.broadcasted_iota(jnp.int32, (seq, nl), 0) // bs
        col = lax.broadcasted_iota(jnp.int32, (seq, nl), 1)
        kaug_ref[:, HEAD_DIM:] = jnp.where(blk == col, MOBA_MASK_BIG, 0.0).astype(BF16)
        vb_ref[...] = v_ref[0].astype(BF16)
        rr = lax.broadcasted_iota(jnp.int32, (V7X_LANES, seq), 0)
        ss = lax.broadcasted_iota(jnp.int32, (V7X_LANES, seq), 1)
        onehot = jnp.where(ss // bs == rr, 1.0, 0.0).astype(BF16)
        tot = jnp.zeros((V7X_LANES, HEAD_DIM), F32)
        for piece in _split3(kf):
            tot = tot + _dot(onehot, piece)
        kmean_ref[...] = tot * (1.0 / bs)

    qf = q_ref[0]
    qb = qf.astype(BF16)

    cand_idx = lax.broadcasted_iota(jnp.int32, (ncand, tq), 0)
    own = i + lax.broadcasted_iota(jnp.int32, (ncand, tq), 1) // bs
    gate = jnp.where(cand_idx < own, _dot3_nt(kmean_ref[...], qf)[:ncand], NEG_INF)
    cand_f = cand_idx.astype(F32)
    picked_t = jnp.zeros((ncand, tq), F32)
    for _ in range(MOBA_TOPK):
        mx = jnp.max(gate, axis=0, keepdims=True)
        first = jnp.min(jnp.where((gate == mx) & (mx > NEG_INF), cand_f, float(ncand)), axis=0, keepdims=True)
        hit = cand_f == first
        picked_t = jnp.where(hit, 1.0, picked_t)
        gate = jnp.where(hit, NEG_INF, gate)
    allowed_t = jnp.where(cand_idx == own, 1.0, picked_t)
    allowed = jnp.concatenate([allowed_t, jnp.zeros((nl - ncand, tq), F32)], axis=0).T
    q_aug = jnp.concatenate([qb, (allowed - 1.0).astype(BF16)], axis=1)

    def raw_scores(first, nblocks):
        start = pl.multiple_of(first * bs, bs)
        return _dot_nt(q_aug, kaug_ref[pl.ds(start, nblocks * bs), :])

    def lane_groups(x):
        return [x[:, g * nl:(g + 1) * nl] for g in range(x.shape[1] // nl)]

    def sweep_past(step):
        wide = MOBA_SWEEP_BLOCKS[0]

        def wide_body(t, c):
            step(t * wide, wide)
            return c

        lax.fori_loop(0, i // wide, wide_body, 0)
        for w in MOBA_SWEEP_BLOCKS[1:]:
            @pl.when((i // w) % 2 == 1)
            def _(w=w):
                step((i // (2 * w)) * (2 * w), w)

    own0 = pl.multiple_of(i * bs, tq)
    row2 = lax.broadcasted_iota(jnp.int32, (tq, tq), 0)
    col2 = lax.broadcasted_iota(jnp.int32, (tq, tq), 1)
    s_own = jnp.where(col2 <= row2, _dot_nt(q_aug, kaug_ref[pl.ds(own0, tq), :]), -MOBA_MASK_BIG)

    m_ref[...] = functools.reduce(jnp.maximum, lane_groups(s_own))

    def max_step(first, nblocks):
        m_ref[...] = functools.reduce(jnp.maximum, [m_ref[...]] + lane_groups(raw_scores(first, nblocks)))

    sweep_past(max_step)
    m_ref[...] = jnp.broadcast_to(jnp.max(m_ref[...], axis=-1, keepdims=True), (tq, nl))

    def accumulate(sc, first_row, init):
        mrow = m_ref[...]
        ps = [jnp.exp2((g - mrow) * exp2_scale) for g in lane_groups(sc)]
        psum = functools.reduce(lambda a, b_: a + b_, ps)
        pv = _dot(jnp.concatenate(ps, axis=1).astype(BF16), vb_ref[pl.ds(first_row, sc.shape[1]), :])
        if init:
            l_ref[...] = psum
            acc_ref[...] = pv
        else:
            l_ref[...] += psum
            acc_ref[...] += pv

    accumulate(s_own, own0, init=True)

    def acc_step(first, nblocks):
        accumulate(raw_scores(first, nblocks), pl.multiple_of(first * bs, bs), init=False)

    sweep_past(acc_step)
    o_ref[0] = (acc_ref[...] / jnp.sum(l_ref[...], axis=-1, keepdims=True)).astype(o_ref.dtype)


def _moba(proj, q_col, k_col, v_col):
    b, s, _ = proj.shape
    h = N_HEADS_HALF
    tq = MOBA_TQ
    assert s % (MOBA_BLOCK * MOBA_SWEEP_BLOCKS[0]) == 0 and s // MOBA_BLOCK <= V7X_LANES
    return pl.pallas_call(
        functools.partial(_moba_kernel, seq=s),
        out_shape=jax.ShapeDtypeStruct((b, s, GROUP_W), BF16),
        grid=(b, h, s // tq),
        in_specs=[pl.BlockSpec((1, tq, HEAD_DIM), lambda bi, hi, i: (bi, i, q_col + hi)),
                  pl.BlockSpec((1, s, HEAD_DIM), lambda bi, hi, i: (bi, 0, k_col + hi)),
                  pl.BlockSpec((1, s, HEAD_DIM), lambda bi, hi, i: (bi, 0, v_col + hi))],
        out_specs=pl.BlockSpec((1, tq, HEAD_DIM), lambda bi, hi, i: (bi, i, hi)),
        scratch_shapes=[pltpu.VMEM((s, HEAD_DIM + V7X_LANES), BF16), pltpu.VMEM((s, HEAD_DIM), BF16),
                        pltpu.VMEM((V7X_LANES, HEAD_DIM), F32),
                        pltpu.VMEM((tq, V7X_LANES), F32), pltpu.VMEM((tq, V7X_LANES), F32),
                        pltpu.VMEM((tq, HEAD_DIM), F32)],
        compiler_params=_cparams("parallel", "parallel", "arbitrary"),
        name="moba_attn",
    )(proj, proj, proj)


DIL_COPY_ROWS = 256
DIL_BLOCKS_PER_STEP = 8


def _dil_kernel(q_ref, k_ref, v_ref, o_ref, qd_ref, kd_ref, vd_ref, obuf_ref, lbuf_ref, *, seq):
    span = DIL_SPAN
    scale = HEAD_DIM ** -0.5
    kd_ref[0:span, :] = jnp.zeros((span, HEAD_DIM), BF16)
    vd_ref[0:span, :] = jnp.zeros((span, HEAD_DIM), BF16)
    qi = lax.broadcasted_iota(jnp.int32, (span, 2 * span), 0)
    ki = lax.broadcasted_iota(jnp.int32, (span, 2 * span), 1)
    in_window = (ki >= qi) & (ki <= qi + span)

    for g, rate in enumerate(DIL_RATES):
        n = seq // rate
        copies_per_res = n // DIL_COPY_ROWS
        nblk = n // span

        def copy(c, cc, rate=rate, copies_per_res=copies_per_res):
            rho = c // copies_per_res
            src = rho + (c % copies_per_res) * (DIL_COPY_ROWS * rate)
            dst = pl.multiple_of(c * DIL_COPY_ROWS, DIL_COPY_ROWS)
            if rate == 1:
                idx = pl.ds(src, DIL_COPY_ROWS)
            else:
                idx = pl.ds(src, DIL_COPY_ROWS, stride=rate)
            qd_ref[pl.ds(dst, DIL_COPY_ROWS), :] = q_ref[0, idx, :].astype(BF16)
            kd_ref[pl.ds(span + dst, DIL_COPY_ROWS), :] = k_ref[0, idx, :].astype(BF16)
            vd_ref[pl.ds(span + dst, DIL_COPY_ROWS), :] = v_ref[0, idx, :].astype(BF16)
            return cc

        lax.fori_loop(0, seq // DIL_COPY_ROWS, copy, 0)

        def blocks(step, cc, g=g, rate=rate, nblk=nblk):
            blks = [step * DIL_BLOCKS_PER_STEP + sub for sub in range(DIL_BLOCKS_PER_STEP)]
            r0s = [pl.multiple_of(blk * span, span) for blk in blks]
            raw = [_dot_nt(qd_ref[pl.ds(r0, span), :], kd_ref[pl.ds(r0, 2 * span), :]) for r0 in r0s]
            scs = [jnp.where(in_window & ((blk % nblk > 0) | (ki >= span)), s * scale, NEG_INF)
                   for blk, s in zip(blks, raw)]
            ms = [jnp.max(sc, axis=-1, keepdims=True) for sc in scs]
            pr = [jnp.exp(sc - m) for sc, m in zip(scs, ms)]
            dens = [jnp.sum(p, axis=-1, keepdims=True) for p in pr]
            pvs = [_dot(p.astype(BF16), vd_ref[pl.ds(r0, 2 * span), :]) for p, r0 in zip(pr, r0s)]
            for blk, pv, m, den in zip(blks, pvs, ms, dens):
                dst = blk // nblk + (blk % nblk) * (span * rate)
                idx = pl.ds(dst, span) if rate == 1 else pl.ds(dst, span, stride=rate)
                obuf_ref[g, idx, :] = pv / den
                lbuf_ref[g, idx, :] = jnp.broadcast_to(m + jnp.log(den), (span, HEAD_DIM))
            return cc

        lax.fori_loop(0, seq // span // DIL_BLOCKS_PER_STEP, blocks, 0)

    def merge(t, c):
        r0 = pl.multiple_of(t * DIL_COPY_ROWS, DIL_COPY_ROWS)
        sl = pl.ds(r0, DIL_COPY_ROWS)
        ls = [lbuf_ref[g, sl, :] for g in range(len(DIL_RATES))]
        mx = functools.reduce(jnp.maximum, ls)
        es = [jnp.exp(l - mx) for l in ls]
        num = functools.reduce(lambda a, b_: a + b_, [e * obuf_ref[g, sl, :] for g, e in enumerate(es)])
        den = functools.reduce(lambda a, b_: a + b_, es)
        o_ref[0, sl, :] = (num / den).astype(o_ref.dtype)
        return c

    lax.fori_loop(0, seq // DIL_COPY_ROWS, merge, 0)


def _dilated(proj, q_col, k_col, v_col):
    b, s, _ = proj.shape
    h = N_HEADS_HALF
    ng = len(DIL_RATES)
    full = lambda col: pl.BlockSpec((1, s, HEAD_DIM), lambda bi, hi: (bi, 0, col + hi))
    return pl.pallas_call(
        functools.partial(_dil_kernel, seq=s),
        out_shape=jax.ShapeDtypeStruct((b, s, GROUP_W), BF16),
        grid=(b, h),
        in_specs=[full(q_col), full(k_col), full(v_col)],
        out_specs=pl.BlockSpec((1, s, HEAD_DIM), lambda bi, hi: (bi, 0, hi)),
        scratch_shapes=[pltpu.VMEM((s, HEAD_DIM), BF16),
                        pltpu.VMEM((s + DIL_SPAN, HEAD_DIM), BF16),
                        pltpu.VMEM((s + DIL_SPAN, HEAD_DIM), BF16),
                        pltpu.VMEM((ng, s, HEAD_DIM), F32), pltpu.VMEM((ng, s, HEAD_DIM), F32)],
        compiler_params=_cparams("parallel", "parallel"),
        name="dilated_attn",
    )(proj, proj, proj)


def _residual_epilogue(y, h_ref, gpost_ref, gnext_ref, h_out_ref, u_out_ref):
    hn = h_ref[...] + _rms(y, gpost_ref[...])
    h_out_ref[...] = hn
    if u_out_ref is not None:
        u_out_ref[...] = _rms(hn, gnext_ref[...]).astype(u_out_ref.dtype)


def _outproj_kernel(a_ref, b_ref, wa_ref, wb_ref, h_ref, gpost_ref, gnext_ref, h_out_ref, u_out_ref):
    y = _dot(a_ref[...], wa_ref[...]) + _dot(b_ref[...], wb_ref[...])
    _residual_epilogue(y, h_ref, gpost_ref, gnext_ref, h_out_ref, u_out_ref)


def _outproj(a, b, wa, wb, h, g_post, g_next, tm=512):
    m, d = h.shape
    ka = a.shape[1]
    kb = b.shape[1]
    rowblk = lambda w: pl.BlockSpec((tm, w), lambda i: (i, 0))
    const = lambda r, c: pl.BlockSpec((r, c), lambda i: (0, 0))
    return pl.pallas_call(
        _outproj_kernel,
        out_shape=(jax.ShapeDtypeStruct((m, d), F32), jax.ShapeDtypeStruct((m, d), BF16)),
        grid=(m // tm,),
        in_specs=[rowblk(ka), rowblk(kb), const(ka, d), const(kb, d), rowblk(d), const(1, d), const(1, d)],
        out_specs=(rowblk(d), rowblk(d)),
        compiler_params=_cparams("parallel"),
        name="outproj_residual",
    )(a, b, wa, wb, h, g_post.reshape(1, d), g_next.reshape(1, d))


def _ffn_kernel(u_ref, wg_ref, wu_ref, wd_ref, h_ref, gpost_ref, gnext_ref, *rest, emit_next):
    if emit_next:
        h_out_ref, u_out_ref, acc_ref = rest
    else:
        h_out_ref, acc_ref = rest
        u_out_ref = None
    f = pl.program_id(1)

    @pl.when(f == 0)
    def _():
        acc_ref[...] = jnp.zeros_like(acc_ref)

    u = u_ref[...]
    gate = _dot(u, wg_ref[...])
    up = _dot(u, wu_ref[...])
    act = (gate * _sigmoid(gate) * up).astype(BF16)
    acc_ref[...] += _dot(act, wd_ref[...])

    @pl.when(f == pl.num_programs(1) - 1)
    def _():
        _residual_epilogue(acc_ref[...], h_ref, gpost_ref, gnext_ref, h_out_ref, u_out_ref)


def _ffn(u, wg, wu, wd, h, g_post, g_next, emit_next, tm=512, tf=512):
    m, d = h.shape
    ff = wg.shape[1]
    rowblk = pl.BlockSpec((tm, d), lambda i, f: (i, 0))
    const = pl.BlockSpec((1, d), lambda i, f: (0, 0))
    out_shape = [jax.ShapeDtypeStruct((m, d), F32)]
    out_specs = [rowblk]
    if emit_next:
        out_shape.append(jax.ShapeDtypeStruct((m, d), BF16))
        out_specs.append(rowblk)
    res = pl.pallas_call(
        functools.partial(_ffn_kernel, emit_next=emit_next),
        out_shape=tuple(out_shape),
        grid=(m // tm, ff // tf),
        in_specs=[rowblk,
                  pl.BlockSpec((d, tf), lambda i, f: (0, f)),
                  pl.BlockSpec((d, tf), lambda i, f: (0, f)),
                  pl.BlockSpec((tf, d), lambda i, f: (f, 0)),
                  rowblk, const, const],
        out_specs=tuple(out_specs),
        scratch_shapes=[pltpu.VMEM((tm, d), F32)],
        compiler_params=_cparams("parallel", "arbitrary"),
        name="swiglu_residual",
    )(u, wg, wu, wd, h, g_post.reshape(1, d), g_next.reshape(1, d))
    return res if emit_next else (res[0], None)


def _pad_lanes(vec, offset=0):
    out = jnp.zeros((1, V7X_LANES), F32)
    return lax.dynamic_update_slice(out, vec.reshape(1, -1).astype(F32), (0, offset))


def kernel(x, mix_norm_pre, mix_norm_post, ffn_norm_pre, ffn_norm_post, ev_w_in, ev_conv_w, ev_a_log,
           ev_dt_bias, ev_onorm, ev_w_out, od_w_in, od_w_out, ffn_w_gate, ffn_w_up, ffn_w_down):
    b, s, d = x.shape
    m = b * s
    gw = GROUP_W
    nh = N_HEADS_HALF
    h = x.reshape(m, d)

    u = _prenorm(h, mix_norm_pre[0])
    w_in = ev_w_in[0]
    n_sb = 3 * gw
    n_gdn = 4 * gw
    proj_sb = _matmul(u, w_in[:, :n_sb].astype(BF16)).reshape(b, s, n_sb)
    proj_gdn = _matmul(u, w_in[:, n_sb:n_sb + n_gdn].astype(BF16)).reshape(b, s, n_gdn)
    w_ab = jnp.pad(w_in[:, n_sb + n_gdn:], ((0, 0), (0, V7X_LANES - 2 * nh))).astype(BF16)
    proj_ab = _matmul(u, w_ab).reshape(b, s, V7X_LANES)
    o_sb = _sb_attention(proj_sb, 0, nh, 2 * nh)
    o_gdn = _gdn(proj_gdn, proj_ab, ev_conv_w[0], _pad_lanes(ev_a_log[0]), _pad_lanes(ev_dt_bias[0]),
                 ev_onorm[0].reshape(1, HEAD_DIM))
    w_out = ev_w_out[0].astype(BF16)
    h, u = _outproj(o_sb.reshape(m, gw), o_gdn.reshape(m, gw), w_out[:gw], w_out[gw:], h,
                    mix_norm_post[0], ffn_norm_pre[0])
    h, u = _ffn(u, ffn_w_gate[0].astype(BF16), ffn_w_up[0].astype(BF16), ffn_w_down[0].astype(BF16), h,
                ffn_norm_post[0], mix_norm_pre[1], emit_next=True)

    proj_od = _matmul(u, od_w_in[0].astype(BF16)).reshape(b, s, 6 * gw)
    o_moba = _moba(proj_od, 0, nh, 2 * nh)
    o_dil = _dilated(proj_od, 3 * nh, 4 * nh, 5 * nh)
    w_out = od_w_out[0].astype(BF16)
    h, u = _outproj(o_moba.reshape(m, gw), o_dil.reshape(m, gw), w_out[:gw], w_out[gw:], h,
                    mix_norm_post[1], ffn_norm_pre[1])
    h, _ = _ffn(u, ffn_w_gate[1].astype(BF16), ffn_w_up[1].astype(BF16), ffn_w_down[1].astype(BF16), h,
                ffn_norm_post[1], ffn_norm_post[1], emit_next=False)
    return h.reshape(b, s, d)
```

```python
import functools

import jax
import jax.numpy as jnp
from jax import lax
from jax.experimental import pallas as pl
from jax.experimental.pallas import tpu as pltpu

F32 = jnp.float32
BF16 = jnp.bfloat16

HEAD_DIM = 128
N_HEADS_HALF = 8
GROUP_W = N_HEADS_HALF * HEAD_DIM
NORM_EPS = 1e-6
GDN_CHUNK = 64
GDN_CONV = 4
MOBA_BLOCK = 256
MOBA_TOPK = 3
DIL_SPAN = 128
DIL_RATES = (1, 4, 16)

V7X_LANES = 128
V7X_VMEM_LIMIT_BYTES = 56 * 1024 * 1024

NEG_INF = float("-inf")


def _cparams(*sem):
    return pltpu.CompilerParams(dimension_semantics=sem, vmem_limit_bytes=V7X_VMEM_LIMIT_BYTES)


def _dot(a, b):
    return jnp.dot(a, b, preferred_element_type=F32)


def _dot_nt(a, b):
    return lax.dot_general(a, b, (((1,), (1,)), ((), ())), preferred_element_type=F32)


def _dot_tn(a, b):
    return lax.dot_general(a, b, (((0,), (0,)), ((), ())), preferred_element_type=F32)


def _split2(x):
    hi = x.astype(BF16)
    lo = (x - hi.astype(F32)).astype(BF16)
    return hi, lo


def _split3(x):
    p1 = x.astype(BF16)
    r1 = x - p1.astype(F32)
    p2 = r1.astype(BF16)
    p3 = (r1 - p2.astype(F32)).astype(BF16)
    return p1, p2, p3


def _dot3(a, b):
    a1, a2 = _split2(a)
    b1, b2 = _split2(b)
    return _dot(a1, b1) + _dot(a1, b2) + _dot(a2, b1)


def _dot3_nt(a, b):
    a1, a2 = _split2(a)
    b1, b2 = _split2(b)
    return _dot_nt(a1, b1) + _dot_nt(a1, b2) + _dot_nt(a2, b1)


def _softplus(x):
    return jnp.maximum(x, 0.0) + jnp.log1p(jnp.exp(-jnp.abs(x)))


def _sigmoid(x):
    return 1.0 / (1.0 + jnp.exp(-x))


def _rms(x, gain):
    return x * lax.rsqrt(jnp.mean(x * x, axis=-1, keepdims=True) + NORM_EPS) * gain


def _prenorm_kernel(x_ref, g_ref, o_ref):
    o_ref[...] = _rms(x_ref[...], g_ref[...]).astype(o_ref.dtype)


def _prenorm(x2d, gain, tm=512):
    m, d = x2d.shape
    return pl.pallas_call(
        _prenorm_kernel,
        out_shape=jax.ShapeDtypeStruct((m, d), BF16),
        grid=(m // tm,),
        in_specs=[pl.BlockSpec((tm, d), lambda i: (i, 0)),
                  pl.BlockSpec((1, d), lambda i: (0, 0))],
        out_specs=pl.BlockSpec((tm, d), lambda i: (i, 0)),
        compiler_params=_cparams("parallel"),
        name="prenorm",
    )(x2d, gain.reshape(1, d))


def _matmul_kernel(x_ref, w_ref, o_ref):
    o_ref[...] = _dot(x_ref[...], w_ref[...]).astype(o_ref.dtype)


def _matmul(x, w, out_dtype=F32, tm=2048, tn=512):
    m, k = x.shape
    n = w.shape[1]
    tn = min(tn, n)
    return pl.pallas_call(
        _matmul_kernel,
        out_shape=jax.ShapeDtypeStruct((m, n), out_dtype),
        grid=(m // tm, n // tn),
        in_specs=[pl.BlockSpec((tm, k), lambda i, j: (i, 0)),
                  pl.BlockSpec((k, tn), lambda i, j: (0, j))],
        out_specs=pl.BlockSpec((tm, tn), lambda i, j: (i, j)),
        compiler_params=_cparams("parallel", "arbitrary"),
        name="proj_matmul",
    )(x, w)


SB_TQ = 512
SB_PAIR = 2 * V7X_LANES


def _sb_sweep_step(q, kb_ref, vb_ref, acc_ref, carry_ref, suffix_op, start, qpos, masked):
    scale = HEAD_DIM ** -0.5
    log2e = 1.4426950408889634
    nl = V7X_LANES
    pairs = range(SB_TQ // SB_PAIR - 1, -1, -1)
    p0s = [pl.multiple_of(start + pair * SB_PAIR, SB_PAIR) for pair in pairs]
    raws = [_dot_nt(q, kb_ref[pl.ds(p0, SB_PAIR), :]) for p0 in p0s]
    log_sig, pasts, parts = [], [], []
    for p0, raw in zip(p0s, raws):
        z = raw * scale
        sp = jnp.maximum(z, 0.0) + jnp.log(1.0 + jnp.exp2(jnp.abs(raw) * (-scale * log2e)))
        log_sig.append(z - sp)
        if masked:
            past = (p0 + lax.broadcasted_iota(jnp.int32, (SB_TQ, SB_PAIR), 1)) < qpos
            sp = jnp.where(past, sp, 0.0)
            pasts.append(past)
        hi, lo = _split2(sp)
        parts.append((jnp.concatenate([hi[:, nl:], lo[:, nl:]], axis=1),
                      jnp.concatenate([hi[:, :nl], lo[:, :nl]], axis=1)))
    sums = [(_dot(near, suffix_op), _dot(far, suffix_op)) for near, far in parts]
    carry = carry_ref[...]
    betweens = []
    for sr_near, sr_far in sums:
        c_far = carry + sr_near[:, nl:]
        betweens.append(jnp.concatenate([sr_far[:, :nl] + c_far, sr_near[:, :nl] + carry], axis=1))
        carry = c_far + sr_far[:, nl:]
    carry_ref[...] = carry
    ws = []
    for idx, (ls, between) in enumerate(zip(log_sig, betweens)):
        w = jnp.exp2((ls - between) * log2e)
        if masked:
            w = jnp.where(pasts[idx], w, 0.0)
        ws.append(w.astype(BF16))
    acc_ref[...] += sum(_dot(w, vb_ref[pl.ds(p0, SB_PAIR), :]) for w, p0 in zip(ws, p0s))


def _sb_kernel(q_ref, k_ref, v_ref, o_ref, kb_ref, vb_ref, acc_ref, carry_ref):
    i = pl.program_id(2)

    @pl.when(i == 0)
    def _():
        kb_ref[...] = k_ref[0].astype(BF16)
        vb_ref[...] = v_ref[0].astype(BF16)

    q = q_ref[0].astype(BF16)
    acc_ref[...] = jnp.zeros_like(acc_ref)
    carry_ref[...] = jnp.zeros_like(carry_ref)

    kk = lax.broadcasted_iota(jnp.int32, (SB_PAIR, SB_PAIR), 0) % V7X_LANES
    cc = lax.broadcasted_iota(jnp.int32, (SB_PAIR, SB_PAIR), 1)
    suffix_op = jnp.where((cc >= V7X_LANES) | (kk > cc), 1.0, 0.0).astype(BF16)

    qpos = i * SB_TQ + lax.broadcasted_iota(jnp.int32, (SB_TQ, SB_PAIR), 0)
    _sb_sweep_step(q, kb_ref, vb_ref, acc_ref, carry_ref, suffix_op, i * SB_TQ, qpos, masked=True)

    def body(t, c):
        _sb_sweep_step(q, kb_ref, vb_ref, acc_ref, carry_ref, suffix_op, (i - t) * SB_TQ, None, masked=False)
        return c

    lax.fori_loop(1, i + 1, body, 0)
    o_ref[0] = acc_ref[...].astype(o_ref.dtype)


def _sb_attention(proj, q_col, k_col, v_col):
    b, s, _ = proj.shape
    h = N_HEADS_HALF
    return pl.pallas_call(
        _sb_kernel,
        out_shape=jax.ShapeDtypeStruct((b, s, GROUP_W), BF16),
        grid=(b, h, s // SB_TQ),
        in_specs=[pl.BlockSpec((1, SB_TQ, HEAD_DIM), lambda bi, hi, i: (bi, i, q_col + hi)),
                  pl.BlockSpec((1, s, HEAD_DIM), lambda bi, hi, i: (bi, 0, k_col + hi)),
                  pl.BlockSpec((1, s, HEAD_DIM), lambda bi, hi, i: (bi, 0, v_col + hi))],
        out_specs=pl.BlockSpec((1, SB_TQ, HEAD_DIM), lambda bi, hi, i: (bi, i, hi)),
        scratch_shapes=[pltpu.VMEM((s, HEAD_DIM), BF16), pltpu.VMEM((s, HEAD_DIM), BF16),
                        pltpu.VMEM((SB_TQ, HEAD_DIM), F32), pltpu.VMEM((SB_TQ, HEAD_DIM), F32)],
        compiler_params=_cparams("parallel", "parallel", "arbitrary"),
        name="stickbreak_attn",
    )(proj, proj, proj)


GDN_ROWS = 512
GDN_GROUP = 256
GDN_HALO = 8
GDN_HEADS_PER_STEP = 4


def _gdn_prep(hd, xq_ref, xk_ref, xv_ref, ab_ref, wq_ref, wk_ref, wv_ref, alog_ref, dtb_ref, xbuf_ref):
    h = pl.program_id(1) * GDN_HEADS_PER_STEP + hd
    hl = slice(hd * HEAD_DIM, (hd + 1) * HEAD_DIM)
    rows = GDN_ROWS
    c = GDN_CHUNK

    conv = []
    for idx, (x_ref, cw_ref) in enumerate(((xq_ref, wq_ref), (xk_ref, wk_ref), (xv_ref, wv_ref))):
        xbuf_ref[hd, idx, GDN_HALO:GDN_HALO + rows, :] = x_ref[0, :, hl]
        y = jnp.zeros((rows, HEAD_DIM), F32)
        for tap in range(GDN_CONV):
            shift = GDN_CONV - 1 - tap
            y = y + cw_ref[tap:tap + 1, hl] * xbuf_ref[hd, idx, GDN_HALO - shift:GDN_HALO - shift + rows, :]
        xbuf_ref[hd, idx, 0:GDN_HALO, :] = xbuf_ref[hd, idx, rows:rows + GDN_HALO, :]
        conv.append(y * _sigmoid(y))
    yq, yk, yv = conv
    q = yq * lax.rsqrt(jnp.sum(yq * yq, axis=-1, keepdims=True) + NORM_EPS) * (HEAD_DIM ** -0.5)
    k = yk * lax.rsqrt(jnp.sum(yk * yk, axis=-1, keepdims=True) + NORM_EPS)
    v = yv

    ab = ab_ref[0]
    g_all = -jnp.exp(alog_ref[...]) * _softplus(ab + dtb_ref[...])
    beta_all = _sigmoid(ab)
    ab_lane = lax.broadcasted_iota(jnp.int32, (rows, V7X_LANES), 1)
    g_col = jnp.sum(jnp.where(ab_lane == h, g_all, 0.0), axis=-1, keepdims=True)
    beta = jnp.sum(jnp.where(ab_lane == h + N_HEADS_HALF, beta_all, 0.0), axis=-1, keepdims=True)
    g_rep = jnp.broadcast_to(g_col, (rows, HEAD_DIM))
    ri = lax.broadcasted_iota(jnp.int32, (GDN_GROUP, GDN_GROUP), 0)
    ci = lax.broadcasted_iota(jnp.int32, (GDN_GROUP, GDN_GROUP), 1)
    cum_op = jnp.where((ri // c == ci // c) & (ci <= ri), 1.0, 0.0).astype(BF16)
    gc = jnp.concatenate(
        [sum(_dot(cum_op, piece) for piece in _split3(g_rep[r0:r0 + GDN_GROUP]))
         for r0 in range(0, rows, GDN_GROUP)], axis=0)

    gc3 = gc.reshape(rows // c, c, HEAD_DIM)
    gl = jnp.broadcast_to(gc3[:, c - 1:c, :], (rows // c, c, HEAD_DIM)).reshape(rows, HEAD_DIM)
    eg = jnp.exp(gc)
    egl = jnp.exp(gl)
    kd = k * jnp.exp(gl - gc)
    qg = q * eg
    kb = k * beta
    rhs_uw = jnp.concatenate([v * beta, kb * eg], axis=1)
    return dict(q=q, k=k, kb=kb, kd=kd, qg=qg, gc=gc, egl=egl, rhs_uw=rhs_uw)


def _gdn_kernel(xq_ref, xk_ref, xv_ref, z_ref, ab_ref, wq_ref, wk_ref, wv_ref, alog_ref, dtb_ref,
                onorm_ref, o_ref, xbuf_ref, state_ref, u_ref, w_ref, vnew_ref, ointer_ref):
    rows = GDN_ROWS
    c = GDN_CHUNK
    nch = rows // c
    heads = range(GDN_HEADS_PER_STEP)
    groups = [(hd, slice(r0, r0 + GDN_GROUP)) for hd in heads for r0 in range(0, rows, GDN_GROUP)]

    @pl.when(pl.program_id(2) == 0)
    def _():
        state_ref[...] = jnp.zeros_like(state_ref)
        xbuf_ref[:, :, 0:GDN_HALO, :] = jnp.zeros((GDN_HEADS_PER_STEP, 3, GDN_HALO, HEAD_DIM), F32)

    hv = [_gdn_prep(hd, xq_ref, xk_ref, xv_ref, ab_ref, wq_ref, wk_ref, wv_ref, alog_ref, dtb_ref, xbuf_ref)
          for hd in heads]

    lane = lax.broadcasted_iota(jnp.int32, (GDN_GROUP, V7X_LANES), 1)
    gi = lax.broadcasted_iota(jnp.int32, (GDN_GROUP, GDN_GROUP), 0)
    gj = lax.broadcasted_iota(jnp.int32, (GDN_GROUP, GDN_GROUP), 1)
    same = (gi // c) == (gj // c)
    lower_incl = same & (gj <= gi)
    lower_strict = same & (gj < gi)
    eye = jnp.where(gi == gj, 1.0, 0.0)

    ps, xs, intras = [], [], []
    for hd, sl in groups:
        p1, p2, p3 = (piece.astype(F32) for piece in _split3(hv[hd]["gc"][sl]))
        lhs = jnp.where(lane == 0, p1, jnp.where(lane == 1, p2, jnp.where(lane == 2, p3,
              jnp.where(lane < 6, 1.0, 0.0))))
        rhs = jnp.where(lane < 3, 1.0, jnp.where(lane == 3, -p1, jnp.where(lane == 4, -p2,
              jnp.where(lane == 5, -p3, 0.0))))
        diff = _dot_nt(lhs.astype(BF16), rhs.astype(BF16))
        decay = jnp.exp(jnp.where(lower_incl, diff, 0.0))
        kgb = hv[hd]["k"][sl].astype(BF16)
        a_mat = jnp.where(lower_strict, _dot_nt(hv[hd]["kb"][sl].astype(BF16), kgb) * decay, 0.0)
        intras.append(jnp.where(lower_incl, _dot_nt(hv[hd]["q"][sl].astype(BF16), kgb) * decay, 0.0).astype(BF16))
        ps.append(-a_mat)
        xs.append(eye - a_mat)

    for _ in range(5):
        pbs = [p.astype(BF16) for p in ps]
        ps = [_dot(pb, pb) for pb in pbs]
        xs = [x + _dot(x.astype(BF16), p.astype(BF16)) for x, p in zip(xs, ps)]
    for (hd, sl), x in zip(groups, xs):
        uw = _dot(x.astype(BF16), hv[hd]["rhs_uw"][sl].astype(BF16))
        u_ref[hd, sl, :] = uw[:, :HEAD_DIM]
        w_ref[hd, sl, :] = uw[:, HEAD_DIM:]

    chunk = lambda ch: slice(ch * c, (ch + 1) * c)
    trans = [[None] * nch for _ in heads]
    drive = [[None] * nch for _ in heads]
    for ch in range(nch):
        for hd in heads:
            kdb = hv[hd]["kd"][chunk(ch)].astype(BF16)
            trans[hd][ch] = _dot_tn(kdb, w_ref[hd, chunk(ch), :].astype(BF16)).astype(BF16)
            drive[hd][ch] = _dot_tn(kdb, u_ref[hd, chunk(ch), :].astype(BF16))
    st = [state_ref[hd] for hd in heads]
    states = [[None] * nch for _ in heads]
    for ch in range(nch):
        for hd in heads:
            stb = st[hd].astype(BF16)
            states[hd][ch] = stb
            st[hd] = st[hd] * hv[hd]["egl"][ch * c:ch * c + 1, :] - _dot(trans[hd][ch], stb) + drive[hd][ch]
    for hd in heads:
        state_ref[hd] = st[hd]
    for ch in range(nch):
        for hd in heads:
            lhs = jnp.concatenate([w_ref[hd, chunk(ch), :], hv[hd]["qg"][chunk(ch)]], axis=0).astype(BF16)
            ws = _dot(lhs, states[hd][ch])
            vnew_ref[hd, chunk(ch), :] = u_ref[hd, chunk(ch), :] - ws[:c]
            ointer_ref[hd, chunk(ch), :] = ws[c:]

    for (hd, sl), intra in zip(groups, intras):
        hl = slice(hd * HEAD_DIM, (hd + 1) * HEAD_DIM)
        zz = z_ref[0, sl, hl]
        o = ointer_ref[hd, sl, :] + _dot(intra, vnew_ref[hd, sl, :].astype(BF16))
        o_ref[0, sl, hl] = (_rms(o, onorm_ref[...]) * (zz * _sigmoid(zz))).astype(o_ref.dtype)


def _gdn(proj, ab, conv_w, alog_row, dtb_row, onorm_row):
    b, s, _ = proj.shape
    nhd = GDN_HEADS_PER_STEP
    wblk = nhd * HEAD_DIM
    hsteps = N_HEADS_HALF // nhd
    row = lambda col: pl.BlockSpec((1, GDN_ROWS, wblk), lambda bi, hi, si: (bi, si, col + hi))
    cw = lambda col: pl.BlockSpec((GDN_CONV, wblk), lambda bi, hi, si: (0, col + hi))
    vec = pl.BlockSpec((1, V7X_LANES), lambda bi, hi, si: (0, 0))
    per_head = lambda *shape: pltpu.VMEM((nhd,) + shape, F32)
    return pl.pallas_call(
        _gdn_kernel,
        out_shape=jax.ShapeDtypeStruct((b, s, GROUP_W), BF16),
        grid=(b, hsteps, s // GDN_ROWS),
        in_specs=[row(0), row(hsteps), row(2 * hsteps), row(3 * hsteps),
                  pl.BlockSpec((1, GDN_ROWS, V7X_LANES), lambda bi, hi, si: (bi, si, 0)),
                  cw(0), cw(hsteps), cw(2 * hsteps), vec, vec, vec],
        out_specs=pl.BlockSpec((1, GDN_ROWS, wblk), lambda bi, hi, si: (bi, si, hi)),
        scratch_shapes=[per_head(3, GDN_ROWS + GDN_HALO, HEAD_DIM),
                        per_head(HEAD_DIM, HEAD_DIM),
                        per_head(GDN_ROWS, HEAD_DIM), per_head(GDN_ROWS, HEAD_DIM),
                        per_head(GDN_ROWS, HEAD_DIM), per_head(GDN_ROWS, HEAD_DIM)],
        compiler_params=_cparams("parallel", "parallel", "arbitrary"),
        name="gated_delta",
    )(proj, proj, proj, proj, ab, conv_w, conv_w, conv_w, alog_row, dtb_row, onorm_row)


MOBA_MASK_BIG = 1e30
MOBA_TQ = 2 * MOBA_BLOCK
MOBA_SWEEP_BLOCKS = (4, 2)


def _moba_kernel(q_ref, k_ref, v_ref, o_ref, kaug_ref, vb_ref, kmean_ref, m_ref, l_ref, acc_ref, sbuf_ref,
                 *, seq):
    bs = MOBA_BLOCK
    tq = MOBA_TQ
    nb = seq // bs
    nl = V7X_LANES
    ncand = -(-nb // 8) * 8
    i = (tq // bs) * pl.program_id(2)
    exp2_scale = (HEAD_DIM ** -0.5) * 1.4426950408889634

    @pl.when(i == 0)
    def _():
        kf = k_ref[0]
        kaug_ref[:, :HEAD_DIM] = kf.astype(BF16)
        blk = lax.broadcasted_iota(jnp.int32, (seq, nl), 0) // bs
        col = lax.broadcasted_iota(jnp.int32, (seq, nl), 1)
        kaug_ref[:, HEAD_DIM:] = jnp.where(blk == col, MOBA_MASK_BIG, 0.0).astype(BF16)
        vb_ref[...] = v_ref[0].astype(BF16)
        rr = lax.broadcasted_iota(jnp.int32, (V7X_LANES, seq), 0)
        ss = lax.broadcasted_iota(jnp.int32, (V7X_LANES, seq), 1)
        onehot = jnp.where(ss // bs == rr, 1.0, 0.0).astype(BF16)
        tot = jnp.zeros((V7X_LANES, HEAD_DIM), F32)
        for piece in _split3(kf):
            tot = tot + _dot(onehot, piece)
        kmean_ref[...] = tot * (1.0 / bs)

    qf = q_ref[0]
    qb = qf.astype(BF16)

    cand_idx = lax.broadcasted_iota(jnp.int32, (ncand, tq), 0)
    own = i + lax.broadcasted_iota(jnp.int32, (ncand, tq), 1) // bs
    gate = jnp.where(cand_idx < own, _dot3_nt(kmean_ref[...], qf)[:ncand], NEG_INF)
    cand_f = cand_idx.astype(F32)
    picked_t = jnp.zeros((ncand, tq), F32)
    for _ in range(MOBA_TOPK):
        mx = jnp.max(gate, axis=0, keepdims=True)
        first = jnp.min(jnp.where((gate == mx) & (mx > NEG_INF), cand_f, float(ncand)), axis=0, keepdims=True)
        hit = cand_f == first
        picked_t = jnp.where(hit, 1.0, picked_t)
        gate = jnp.where(hit, NEG_INF, gate)
    allowed_t = jnp.where(cand_idx == own, 1.0, picked_t)
    allowed = jnp.concatenate([allowed_t, jnp.zeros((nl - ncand, tq), F32)], axis=0).T
    q_aug = jnp.concatenate([qb, (allowed - 1.0).astype(BF16)], axis=1)

    def raw_scores(first, nblocks):
        start = pl.multiple_of(first * bs, bs)
        return _dot_nt(q_aug, kaug_ref[pl.ds(start, nblocks * bs), :])

    def lane_groups(x):
        return [x[:, g * nl:(g + 1) * nl] for g in range(x.shape[1] // nl)]

    def sweep_past(step):
        wide = MOBA_SWEEP_BLOCKS[0]

        def wide_body(t, c):
            step(t * wide, wide)
            return c

        lax.fori_loop(0, i // wide, wide_body, 0)
        for w in MOBA_SWEEP_BLOCKS[1:]:
            @pl.when((i // w) % 2 == 1)
            def _(w=w):
                step((i // (2 * w)) * (2 * w), w)

    own0 = pl.multiple_of(i * bs, tq)
    row2 = lax.broadcasted_iota(jnp.int32, (tq, tq), 0)
    col2 = lax.broadcasted_iota(jnp.int32, (tq, tq), 1)
    s_own = jnp.where(col2 <= row2, _dot_nt(q_aug, kaug_ref[pl.ds(own0, tq), :]), -MOBA_MASK_BIG)

    m_ref[...] = functools.reduce(jnp.maximum, lane_groups(s_own))

    def max_step(first, nblocks):
        sc = raw_scores(first, nblocks)
        sbuf_ref[:, pl.ds(pl.multiple_of(first * bs, bs), nblocks * bs)] = sc
        m_ref[...] = functools.reduce(jnp.maximum, [m_ref[...]] + lane_groups(sc))

    sweep_past(max_step)
    m_ref[...] = jnp.broadcast_to(jnp.max(m_ref[...], axis=-1, keepdims=True), (tq, nl))

    def accumulate(sc, first_row, init):
        mrow = m_ref[...]
        ps = [jnp.exp2((g - mrow) * exp2_scale) for g in lane_groups(sc)]
        psum = functools.reduce(lambda a, b_: a + b_, ps)
        pv = _dot(jnp.concatenate(ps, axis=1).astype(BF16), vb_ref[pl.ds(first_row, sc.shape[1]), :])
        if init:
            l_ref[...] = psum
            acc_ref[...] = pv
        else:
            l_ref[...] += psum
            acc_ref[...] += pv

    accumulate(s_own, own0, init=True)

    def acc_step(first, nblocks):
        start = pl.multiple_of(first * bs, bs)
        accumulate(sbuf_ref[:, pl.ds(start, nblocks * bs)], start, init=False)

    sweep_past(acc_step)
    o_ref[0] = (acc_ref[...] / jnp.sum(l_ref[...], axis=-1, keepdims=True)).astype(o_ref.dtype)


def _moba(proj, q_col, k_col, v_col):
    b, s, _ = proj.shape
    h = N_HEADS_HALF
    tq = MOBA_TQ
    assert s % (MOBA_BLOCK * MOBA_SWEEP_BLOCKS[0]) == 0 and s // MOBA_BLOCK <= V7X_LANES
    return pl.pallas_call(
        functools.partial(_moba_kernel, seq=s),
        out_shape=jax.ShapeDtypeStruct((b, s, GROUP_W), BF16),
        grid=(b, h, s // tq),
        in_specs=[pl.BlockSpec((1, tq, HEAD_DIM), lambda bi, hi, i: (bi, i, q_col + hi)),
                  pl.BlockSpec((1, s, HEAD_DIM), lambda bi, hi, i: (bi, 0, k_col + hi)),
                  pl.BlockSpec((1, s, HEAD_DIM), lambda bi, hi, i: (bi, 0, v_col + hi))],
        out_specs=pl.BlockSpec((1, tq, HEAD_DIM), lambda bi, hi, i: (bi, i, hi)),
        scratch_shapes=[pltpu.VMEM((s, HEAD_DIM + V7X_LANES), BF16), pltpu.VMEM((s, HEAD_DIM), BF16),
                        pltpu.VMEM((V7X_LANES, HEAD_DIM), F32),
                        pltpu.VMEM((tq, V7X_LANES), F32), pltpu.VMEM((tq, V7X_LANES), F32),
                        pltpu.VMEM((tq, HEAD_DIM), F32), pltpu.VMEM((tq, s), F32)],
        compiler_params=_cparams("parallel", "parallel", "arbitrary"),
        name="moba_attn",
    )(proj, proj, proj)


DIL_COPY_ROWS = 256
DIL_BLOCKS_PER_STEP = 8


def _dil_kernel(q_ref, k_ref, v_ref, o_ref, qd_ref, kd_ref, vd_ref, obuf_ref, lbuf_ref, *, seq):
    span = DIL_SPAN
    scale = HEAD_DIM ** -0.5
    kd_ref[0:span, :] = jnp.zeros((span, HEAD_DIM), BF16)
    vd_ref[0:span, :] = jnp.zeros((span, HEAD_DIM), BF16)
    qi = lax.broadcasted_iota(jnp.int32, (span, 2 * span), 0)
    ki = lax.broadcasted_iota(jnp.int32, (span, 2 * span), 1)
    in_window = (ki >= qi) & (ki <= qi + span)

    for g, rate in enumerate(DIL_RATES):
        n = seq // rate
        copies_per_res = n // DIL_COPY_ROWS
        nblk = n // span

        def copy(c, cc, rate=rate, copies_per_res=copies_per_res):
            rho = c // copies_per_res
            src = rho + (c % copies_per_res) * (DIL_COPY_ROWS * rate)
            dst = pl.multiple_of(c * DIL_COPY_ROWS, DIL_COPY_ROWS)
            if rate == 1:
                idx = pl.ds(src, DIL_COPY_ROWS)
            else:
                idx = pl.ds(src, DIL_COPY_ROWS, stride=rate)
            qd_ref[pl.ds(dst, DIL_COPY_ROWS), :] = q_ref[0, idx, :].astype(BF16)
            kd_ref[pl.ds(span + dst, DIL_COPY_ROWS), :] = k_ref[0, idx, :].astype(BF16)
            vd_ref[pl.ds(span + dst, DIL_COPY_ROWS), :] = v_ref[0, idx, :].astype(BF16)
            return cc

        lax.fori_loop(0, seq // DIL_COPY_ROWS, copy, 0)

        def blocks(step, cc, g=g, rate=rate, nblk=nblk):
            blks = [step * DIL_BLOCKS_PER_STEP + sub for sub in range(DIL_BLOCKS_PER_STEP)]
            r0s = [pl.multiple_of(blk * span, span) for blk in blks]
            raw = [_dot_nt(qd_ref[pl.ds(r0, span), :], kd_ref[pl.ds(r0, 2 * span), :]) for r0 in r0s]
            scs = [jnp.where(in_window & ((blk % nblk > 0) | (ki >= span)), s * scale, NEG_INF)
                   for blk, s in zip(blks, raw)]
            ms = [jnp.max(sc, axis=-1, keepdims=True) for sc in scs]
            pr = [jnp.exp(sc - m) for sc, m in zip(scs, ms)]
            dens = [jnp.sum(p, axis=-1, keepdims=True) for p in pr]
            pvs = [_dot(p.astype(BF16), vd_ref[pl.ds(r0, 2 * span), :]) for p, r0 in zip(pr, r0s)]
            for blk, pv, m, den in zip(blks, pvs, ms, dens):
                dst = blk // nblk + (blk % nblk) * (span * rate)
                idx = pl.ds(dst, span) if rate == 1 else pl.ds(dst, span, stride=rate)
                obuf_ref[g, idx, :] = pv / den
                lbuf_ref[g, idx, :] = jnp.broadcast_to(m + jnp.log(den), (span, HEAD_DIM))
            return cc

        lax.fori_loop(0, seq // span // DIL_BLOCKS_PER_STEP, blocks, 0)

    def merge(t, c):
        r0 = pl.multiple_of(t * DIL_COPY_ROWS, DIL_COPY_ROWS)
        sl = pl.ds(r0, DIL_COPY_ROWS)
        ls = [lbuf_ref[g, sl, :] for g in range(len(DIL_RATES))]
        mx = functools.reduce(jnp.maximum, ls)
        es = [jnp.exp(l - mx) for l in ls]
        num = functools.reduce(lambda a, b_: a + b_, [e * obuf_ref[g, sl, :] for g, e in enumerate(es)])
        den = functools.reduce(lambda a, b_: a + b_, es)
        o_ref[0, sl, :] = (num / den).astype(o_ref.dtype)
        return c

    lax.fori_loop(0, seq // DIL_COPY_ROWS, merge, 0)


def _dilated(proj, q_col, k_col, v_col):
    b, s, _ = proj.shape
    h = N_HEADS_HALF
    ng = len(DIL_RATES)
    full = lambda col: pl.BlockSpec((1, s, HEAD_DIM), lambda bi, hi: (bi, 0, col + hi))
    return pl.pallas_call(
        functools.partial(_dil_kernel, seq=s),
        out_shape=jax.ShapeDtypeStruct((b, s, GROUP_W), BF16),
        grid=(b, h),
        in_specs=[full(q_col), full(k_col), full(v_col)],
        out_specs=pl.BlockSpec((1, s, HEAD_DIM), lambda bi, hi: (bi, 0, hi)),
        scratch_shapes=[pltpu.VMEM((s, HEAD_DIM), BF16),
                        pltpu.VMEM((s + DIL_SPAN, HEAD_DIM), BF16),
                        pltpu.VMEM((s + DIL_SPAN, HEAD_DIM), BF16),
                        pltpu.VMEM((ng, s, HEAD_DIM), F32), pltpu.VMEM((ng, s, HEAD_DIM), F32)],
        compiler_params=_cparams("parallel", "parallel"),
        name="dilated_attn",
    )(proj, proj, proj)


def _residual_epilogue(y, h_ref, gpost_ref, gnext_ref, h_out_ref, u_out_ref):
    hn = h_ref[...] + _rms(y, gpost_ref[...])
    h_out_ref[...] = hn
    if u_out_ref is not None:
        u_out_ref[...] = _rms(hn, gnext_ref[...]).astype(u_out_ref.dtype)


def _outproj_kernel(a_ref, b_ref, wa_ref, wb_ref, h_ref, gpost_ref, gnext_ref, h_out_ref, u_out_ref):
    y = _dot(a_ref[...], wa_ref[...]) + _dot(b_ref[...], wb_ref[...])
    _residual_epilogue(y, h_ref, gpost_ref, gnext_ref, h_out_ref, u_out_ref)


def _outproj(a, b, wa, wb, h, g_post, g_next, tm=512):
    m, d = h.shape
    ka = a.shape[1]
    kb = b.shape[1]
    rowblk = lambda w: pl.BlockSpec((tm, w), lambda i: (i, 0))
    const = lambda r, c: pl.BlockSpec((r, c), lambda i: (0, 0))
    return pl.pallas_call(
        _outproj_kernel,
        out_shape=(jax.ShapeDtypeStruct((m, d), F32), jax.ShapeDtypeStruct((m, d), BF16)),
        grid=(m // tm,),
        in_specs=[rowblk(ka), rowblk(kb), const(ka, d), const(kb, d), rowblk(d), const(1, d), const(1, d)],
        out_specs=(rowblk(d), rowblk(d)),
        compiler_params=_cparams("parallel"),
        name="outproj_residual",
    )(a, b, wa, wb, h, g_post.reshape(1, d), g_next.reshape(1, d))


def _ffn_kernel(u_ref, wg_ref, wu_ref, wd_ref, h_ref, gpost_ref, gnext_ref, *rest, emit_next):
    if emit_next:
        h_out_ref, u_out_ref, acc_ref = rest
    else:
        h_out_ref, acc_ref = rest
        u_out_ref = None
    f = pl.program_id(1)

    @pl.when(f == 0)
    def _():
        acc_ref[...] = jnp.zeros_like(acc_ref)

    u = u_ref[...]
    gate = _dot(u, wg_ref[...])
    up = _dot(u, wu_ref[...])
    act = (gate * _sigmoid(gate) * up).astype(BF16)
    acc_ref[...] += _dot(act, wd_ref[...])

    @pl.when(f == pl.num_programs(1) - 1)
    def _():
        _residual_epilogue(acc_ref[...], h_ref, gpost_ref, gnext_ref, h_out_ref, u_out_ref)


def _ffn(u, wg, wu, wd, h, g_post, g_next, emit_next, tm=512, tf=512):
    m, d = h.shape
    ff = wg.shape[1]
    rowblk = pl.BlockSpec((tm, d), lambda i, f: (i, 0))
    const = pl.BlockSpec((1, d), lambda i, f: (0, 0))
    out_shape = [jax.ShapeDtypeStruct((m, d), F32)]
    out_specs = [rowblk]
    if emit_next:
        out_shape.append(jax.ShapeDtypeStruct((m, d), BF16))
        out_specs.append(rowblk)
    res = pl.pallas_call(
        functools.partial(_ffn_kernel, emit_next=emit_next),
        out_shape=tuple(out_shape),
        grid=(m // tm, ff // tf),
        in_specs=[rowblk,
                  pl.BlockSpec((d, tf), lambda i, f: (0, f)),
                  pl.BlockSpec((d, tf), lambda i, f: (0, f)),
                  pl.BlockSpec((tf, d), lambda i, f: (f, 0)),
                  rowblk, const, const],
        out_specs=tuple(out_specs),
        scratch_shapes=[pltpu.VMEM((tm, d), F32)],
        compiler_params=_cparams("parallel", "arbitrary"),
        name="swiglu_residual",
    )(u, wg, wu, wd, h, g_post.reshape(1, d), g_next.reshape(1, d))
    return res if emit_next else (res[0], None)


def _pad_lanes(vec, offset=0):
    out = jnp.zeros((1, V7X_LANES), F32)
    return lax.dynamic_update_slice(out, vec.reshape(1, -1).astype(F32), (0, offset))


def kernel(x, mix_norm_pre, mix_norm_post, ffn_norm_pre, ffn_norm_post, ev_w_in, ev_conv_w, ev_a_log,
           ev_dt_bias, ev_onorm, ev_w_out, od_w_in, od_w_out, ffn_w_gate, ffn_w_up, ffn_w_down):
    b, s, d = x.shape
    m = b * s
    gw = GROUP_W
    nh = N_HEADS_HALF
    h = x.reshape(m, d)

    u = _prenorm(h, mix_norm_pre[0])
    w_in = ev_w_in[0]
    n_sb = 3 * gw
    n_gdn = 4 * gw
    proj_sb = _matmul(u, w_in[:, :n_sb].astype(BF16)).reshape(b, s, n_sb)
    proj_gdn = _matmul(u, w_in[:, n_sb:n_sb + n_gdn].astype(BF16)).reshape(b, s, n_gdn)
    w_ab = jnp.pad(w_in[:, n_sb + n_gdn:], ((0, 0), (0, V7X_LANES - 2 * nh))).astype(BF16)
    proj_ab = _matmul(u, w_ab).reshape(b, s, V7X_LANES)
    o_sb = _sb_attention(proj_sb, 0, nh, 2 * nh)
    o_gdn = _gdn(proj_gdn, proj_ab, ev_conv_w[0], _pad_lanes(ev_a_log[0]), _pad_lanes(ev_dt_bias[0]),
                 ev_onorm[0].reshape(1, HEAD_DIM))
    w_out = ev_w_out[0].astype(BF16)
    h, u = _outproj(o_sb.reshape(m, gw), o_gdn.reshape(m, gw), w_out[:gw], w_out[gw:], h,
                    mix_norm_post[0], ffn_norm_pre[0])
    h, u = _ffn(u, ffn_w_gate[0].astype(BF16), ffn_w_up[0].astype(BF16), ffn_w_down[0].astype(BF16), h,
                ffn_norm_post[0], mix_norm_pre[1], emit_next=True)

    proj_od = _matmul(u, od_w_in[0].astype(BF16)).reshape(b, s, 6 * gw)
    o_moba = _moba(proj_od, 0, nh, 2 * nh)
    o_dil = _dilated(proj_od, 3 * nh, 4 * nh, 5 * nh)
    w_out = od_w_out[0].astype(BF16)
    h, u = _outproj(o_moba.reshape(m, gw), o_dil.reshape(m, gw), w_out[:gw], w_out[gw:], h,
                    mix_norm_post[1], ffn_norm_pre[1])
    h, _ = _ffn(u, ffn_w_gate[1].astype(BF16), ffn_w_up[1].astype(BF16), ffn_w_down[1].astype(BF16), h,
                ffn_norm_post[1], ffn_norm_post[1], emit_next=False)
    return h.reshape(b, s, d)
```

```python
import functools

import jax
import jax.numpy as jnp
from jax import lax
from jax.experimental import pallas as pl
from jax.experimental.pallas import tpu as pltpu

F32 = jnp.float32
BF16 = jnp.bfloat16

HEAD_DIM = 128
N_HEADS_HALF = 8
GROUP_W = N_HEADS_HALF * HEAD_DIM
NORM_EPS = 1e-6
GDN_CHUNK = 64
GDN_CONV = 4
MOBA_BLOCK = 256
MOBA_TOPK = 3
DIL_SPAN = 128
DIL_RATES = (1, 4, 16)

V7X_LANES = 128
V7X_VMEM_LIMIT_BYTES = 56 * 1024 * 1024

NEG_INF = float("-inf")


def _cparams(*sem):
    return pltpu.CompilerParams(dimension_semantics=sem, vmem_limit_bytes=V7X_VMEM_LIMIT_BYTES)


def _dot(a, b):
    return jnp.dot(a, b, preferred_element_type=F32)


def _dot_nt(a, b):
    return lax.dot_general(a, b, (((1,), (1,)), ((), ())), preferred_element_type=F32)


def _dot_tn(a, b):
    return lax.dot_general(a, b, (((0,), (0,)), ((), ())), preferred_element_type=F32)


def _split2(x):
    hi = x.astype(BF16)
    lo = (x - hi.astype(F32)).astype(BF16)
    return hi, lo


def _split3(x):
    p1 = x.astype(BF16)
    r1 = x - p1.astype(F32)
    p2 = r1.astype(BF16)
    p3 = (r1 - p2.astype(F32)).astype(BF16)
    return p1, p2, p3


def _dot3(a, b):
    a1, a2 = _split2(a)
    b1, b2 = _split2(b)
    return _dot(a1, b1) + _dot(a1, b2) + _dot(a2, b1)


def _dot3_nt(a, b):
    a1, a2 = _split2(a)
    b1, b2 = _split2(b)
    return _dot_nt(a1, b1) + _dot_nt(a1, b2) + _dot_nt(a2, b1)


def _softplus(x):
    return jnp.maximum(x, 0.0) + jnp.log1p(jnp.exp(-jnp.abs(x)))


def _sigmoid(x):
    return 1.0 / (1.0 + jnp.exp(-x))


def _rms(x, gain):
    return x * lax.rsqrt(jnp.mean(x * x, axis=-1, keepdims=True) + NORM_EPS) * gain


def _prenorm_kernel(x_ref, g_ref, o_ref):
    o_ref[...] = _rms(x_ref[...], g_ref[...]).astype(o_ref.dtype)


def _prenorm(x2d, gain, tm=512):
    m, d = x2d.shape
    return pl.pallas_call(
        _prenorm_kernel,
        out_shape=jax.ShapeDtypeStruct((m, d), BF16),
        grid=(m // tm,),
        in_specs=[pl.BlockSpec((tm, d), lambda i: (i, 0)),
                  pl.BlockSpec((1, d), lambda i: (0, 0))],
        out_specs=pl.BlockSpec((tm, d), lambda i: (i, 0)),
        compiler_params=_cparams("parallel"),
        name="prenorm",
    )(x2d, gain.reshape(1, d))


def _matmul_kernel(x_ref, w_ref, o_ref):
    o_ref[...] = _dot(x_ref[...], w_ref[...]).astype(o_ref.dtype)


def _matmul(x, w, out_dtype=F32, tm=2048, tn=512):
    m, k = x.shape
    n = w.shape[1]
    tn = min(tn, n)
    return pl.pallas_call(
        _matmul_kernel,
        out_shape=jax.ShapeDtypeStruct((m, n), out_dtype),
        grid=(m // tm, n // tn),
        in_specs=[pl.BlockSpec((tm, k), lambda i, j: (i, 0)),
                  pl.BlockSpec((k, tn), lambda i, j: (0, j))],
        out_specs=pl.BlockSpec((tm, tn), lambda i, j: (i, j)),
        compiler_params=_cparams("parallel", "arbitrary"),
        name="proj_matmul",
    )(x, w)


SB_TQ = 512
SB_PAIR = 2 * V7X_LANES


def _sb_sweep_step(q, kb_ref, vb_ref, acc_ref, carry_ref, suffix_op, start, qpos, masked):
    scale = HEAD_DIM ** -0.5
    log2e = 1.4426950408889634
    nl = V7X_LANES
    pairs = range(SB_TQ // SB_PAIR - 1, -1, -1)
    p0s = [pl.multiple_of(start + pair * SB_PAIR, SB_PAIR) for pair in pairs]
    raws = [_dot_nt(q, kb_ref[pl.ds(p0, SB_PAIR), :]) for p0 in p0s]
    log_sig, pasts, parts = [], [], []
    for p0, raw in zip(p0s, raws):
        z = raw * scale
        sp = jnp.maximum(z, 0.0) + jnp.log(1.0 + jnp.exp2(jnp.abs(raw) * (-scale * log2e)))
        log_sig.append(z - sp)
        if masked:
            past = (p0 + lax.broadcasted_iota(jnp.int32, (SB_TQ, SB_PAIR), 1)) < qpos
            sp = jnp.where(past, sp, 0.0)
            pasts.append(past)
        hi, lo = _split2(sp)
        parts.append((jnp.concatenate([hi[:, nl:], lo[:, nl:]], axis=1),
                      jnp.concatenate([hi[:, :nl], lo[:, :nl]], axis=1)))
    sums = [(_dot(near, suffix_op), _dot(far, suffix_op)) for near, far in parts]
    carry = carry_ref[...]
    betweens = []
    for sr_near, sr_far in sums:
        c_far = carry + sr_near[:, nl:]
        betweens.append(jnp.concatenate([sr_far[:, :nl] + c_far, sr_near[:, :nl] + carry], axis=1))
        carry = c_far + sr_far[:, nl:]
    carry_ref[...] = carry
    ws = []
    for idx, (ls, between) in enumerate(zip(log_sig, betweens)):
        w = jnp.exp2((ls - between) * log2e)
        if masked:
            w = jnp.where(pasts[idx], w, 0.0)
        ws.append(w.astype(BF16))
    acc_ref[...] += sum(_dot(w, vb_ref[pl.ds(p0, SB_PAIR), :]) for w, p0 in zip(ws, p0s))


def _sb_kernel(q_ref, k_ref, v_ref, o_ref, kb_ref, vb_ref, acc_ref, carry_ref):
    i = pl.program_id(2)

    @pl.when(i == 0)
    def _():
        kb_ref[...] = k_ref[0].astype(BF16)
        vb_ref[...] = v_ref[0].astype(BF16)

    q = q_ref[0].astype(BF16)
    acc_ref[...] = jnp.zeros_like(acc_ref)
    carry_ref[...] = jnp.zeros_like(carry_ref)

    kk = lax.broadcasted_iota(jnp.int32, (SB_PAIR, SB_PAIR), 0) % V7X_LANES
    cc = lax.broadcasted_iota(jnp.int32, (SB_PAIR, SB_PAIR), 1)
    suffix_op = jnp.where((cc >= V7X_LANES) | (kk > cc), 1.0, 0.0).astype(BF16)

    qpos = i * SB_TQ + lax.broadcasted_iota(jnp.int32, (SB_TQ, SB_PAIR), 0)
    _sb_sweep_step(q, kb_ref, vb_ref, acc_ref, carry_ref, suffix_op, i * SB_TQ, qpos, masked=True)

    def body(t, c):
        _sb_sweep_step(q, kb_ref, vb_ref, acc_ref, carry_ref, suffix_op, (i - t) * SB_TQ, None, masked=False)
        return c

    lax.fori_loop(1, i + 1, body, 0)
    o_ref[0] = acc_ref[...].astype(o_ref.dtype)


def _sb_attention(proj, q_col, k_col, v_col):
    b, s, _ = proj.shape
    h = N_HEADS_HALF
    return pl.pallas_call(
        _sb_kernel,
        out_shape=jax.ShapeDtypeStruct((b, s, GROUP_W), BF16),
        grid=(b, h, s // SB_TQ),
        in_specs=[pl.BlockSpec((1, SB_TQ, HEAD_DIM), lambda bi, hi, i: (bi, i, q_col + hi)),
                  pl.BlockSpec((1, s, HEAD_DIM), lambda bi, hi, i: (bi, 0, k_col + hi)),
                  pl.BlockSpec((1, s, HEAD_DIM), lambda bi, hi, i: (bi, 0, v_col + hi))],
        out_specs=pl.BlockSpec((1, SB_TQ, HEAD_DIM), lambda bi, hi, i: (bi, i, hi)),
        scratch_shapes=[pltpu.VMEM((s, HEAD_DIM), BF16), pltpu.VMEM((s, HEAD_DIM), BF16),
                        pltpu.VMEM((SB_TQ, HEAD_DIM), F32), pltpu.VMEM((SB_TQ, HEAD_DIM), F32)],
        compiler_params=_cparams("parallel", "parallel", "arbitrary"),
        name="stickbreak_attn",
    )(proj, proj, proj)


GDN_ROWS = 512
GDN_GROUP = 256
GDN_HALO = 8
GDN_HEADS_PER_STEP = 4


def _gdn_prep(hd, xq_ref, xk_ref, xv_ref, ab_ref, wq_ref, wk_ref, wv_ref, alog_ref, dtb_ref, xbuf_ref):
    h = pl.program_id(1) * GDN_HEADS_PER_STEP + hd
    hl = slice(hd * HEAD_DIM, (hd + 1) * HEAD_DIM)
    rows = GDN_ROWS
    c = GDN_CHUNK

    conv = []
    for idx, (x_ref, cw_ref) in enumerate(((xq_ref, wq_ref), (xk_ref, wk_ref), (xv_ref, wv_ref))):
        xbuf_ref[hd, idx, GDN_HALO:GDN_HALO + rows, :] = x_ref[0, :, hl]
        y = jnp.zeros((rows, HEAD_DIM), F32)
        for tap in range(GDN_CONV):
            shift = GDN_CONV - 1 - tap
            y = y + cw_ref[tap:tap + 1, hl] * xbuf_ref[hd, idx, GDN_HALO - shift:GDN_HALO - shift + rows, :]
        xbuf_ref[hd, idx, 0:GDN_HALO, :] = xbuf_ref[hd, idx, rows:rows + GDN_HALO, :]
        conv.append(y * _sigmoid(y))
    yq, yk, yv = conv
    q = yq * lax.rsqrt(jnp.sum(yq * yq, axis=-1, keepdims=True) + NORM_EPS) * (HEAD_DIM ** -0.5)
    k = yk * lax.rsqrt(jnp.sum(yk * yk, axis=-1, keepdims=True) + NORM_EPS)
    v = yv

    ab = ab_ref[0]
    g_all = -jnp.exp(alog_ref[...]) * _softplus(ab + dtb_ref[...])
    beta_all = _sigmoid(ab)
    ab_lane = lax.broadcasted_iota(jnp.int32, (rows, V7X_LANES), 1)
    g_col = jnp.sum(jnp.where(ab_lane == h, g_all, 0.0), axis=-1, keepdims=True)
    beta = jnp.sum(jnp.where(ab_lane == h + N_HEADS_HALF, beta_all, 0.0), axis=-1, keepdims=True)
    g_rep = jnp.broadcast_to(g_col, (rows, HEAD_DIM))
    ri = lax.broadcasted_iota(jnp.int32, (GDN_GROUP, GDN_GROUP), 0)
    ci = lax.broadcasted_iota(jnp.int32, (GDN_GROUP, GDN_GROUP), 1)
    cum_op = jnp.where((ri // c == ci // c) & (ci <= ri), 1.0, 0.0).astype(BF16)
    gc = jnp.concatenate(
        [sum(_dot(cum_op, piece) for piece in _split3(g_rep[r0:r0 + GDN_GROUP]))
         for r0 in range(0, rows, GDN_GROUP)], axis=0)

    gc3 = gc.reshape(rows // c, c, HEAD_DIM)
    gl = jnp.broadcast_to(gc3[:, c - 1:c, :], (rows // c, c, HEAD_DIM)).reshape(rows, HEAD_DIM)
    eg = jnp.exp(gc)
    egl = jnp.exp(gl)
    kd = k * jnp.exp(gl - gc)
    qg = q * eg
    kb = k * beta
    rhs_uw = jnp.concatenate([v * beta, kb * eg], axis=1)
    return dict(q=q, k=k, kb=kb, kd=kd, qg=qg, gc=gc, egl=egl, rhs_uw=rhs_uw)


def _gdn_kernel(xq_ref, xk_ref, xv_ref, z_ref, ab_ref, wq_ref, wk_ref, wv_ref, alog_ref, dtb_ref,
                onorm_ref, o_ref, xbuf_ref, state_ref, u_ref, w_ref, vnew_ref, ointer_ref):
    rows = GDN_ROWS
    c = GDN_CHUNK
    nch = rows // c
    heads = range(GDN_HEADS_PER_STEP)
    groups = [(hd, slice(r0, r0 + GDN_GROUP)) for hd in heads for r0 in range(0, rows, GDN_GROUP)]

    @pl.when(pl.program_id(2) == 0)
    def _():
        state_ref[...] = jnp.zeros_like(state_ref)
        xbuf_ref[:, :, 0:GDN_HALO, :] = jnp.zeros((GDN_HEADS_PER_STEP, 3, GDN_HALO, HEAD_DIM), F32)

    hv = [_gdn_prep(hd, xq_ref, xk_ref, xv_ref, ab_ref, wq_ref, wk_ref, wv_ref, alog_ref, dtb_ref, xbuf_ref)
          for hd in heads]

    lane = lax.broadcasted_iota(jnp.int32, (GDN_GROUP, V7X_LANES), 1)
    gi = lax.broadcasted_iota(jnp.int32, (GDN_GROUP, GDN_GROUP), 0)
    gj = lax.broadcasted_iota(jnp.int32, (GDN_GROUP, GDN_GROUP), 1)
    same = (gi // c) == (gj // c)
    lower_incl = same & (gj <= gi)
    lower_strict = same & (gj < gi)
    eye = jnp.where(gi == gj, 1.0, 0.0)

    ps, xs, intras = [], [], []
    for hd, sl in groups:
        p1, p2, p3 = (piece.astype(F32) for piece in _split3(hv[hd]["gc"][sl]))
        lhs = jnp.where(lane == 0, p1, jnp.where(lane == 1, p2, jnp.where(lane == 2, p3,
              jnp.where(lane < 6, 1.0, 0.0))))
        rhs = jnp.where(lane < 3, 1.0, jnp.where(lane == 3, -p1, jnp.where(lane == 4, -p2,
              jnp.where(lane == 5, -p3, 0.0))))
        diff = _dot_nt(lhs.astype(BF16), rhs.astype(BF16))
        decay = jnp.exp(jnp.where(lower_incl, diff, 0.0))
        kgb = hv[hd]["k"][sl].astype(BF16)
        a_mat = jnp.where(lower_strict, _dot_nt(hv[hd]["kb"][sl].astype(BF16), kgb) * decay, 0.0)
        intras.append(jnp.where(lower_incl, _dot_nt(hv[hd]["q"][sl].astype(BF16), kgb) * decay, 0.0).astype(BF16))
        ps.append(-a_mat)
        xs.append(eye - a_mat)

    for _ in range(5):
        pbs = [p.astype(BF16) for p in ps]
        ps = [_dot(pb, pb) for pb in pbs]
        xs = [x + _dot(x.astype(BF16), p.astype(BF16)) for x, p in zip(xs, ps)]
    for (hd, sl), x in zip(groups, xs):
        uw = _dot(x.astype(BF16), hv[hd]["rhs_uw"][sl].astype(BF16))
        u_ref[hd, sl, :] = uw[:, :HEAD_DIM]
        w_ref[hd, sl, :] = uw[:, HEAD_DIM:]

    chunk = lambda ch: slice(ch * c, (ch + 1) * c)
    trans = [[None] * nch for _ in heads]
    drive = [[None] * nch for _ in heads]
    for ch in range(nch):
        for hd in heads:
            kdb = hv[hd]["kd"][chunk(ch)].astype(BF16)
            trans[hd][ch] = _dot_tn(kdb, w_ref[hd, chunk(ch), :].astype(BF16)).astype(BF16)
            drive[hd][ch] = _dot_tn(kdb, u_ref[hd, chunk(ch), :].astype(BF16))
    st = [state_ref[hd] for hd in heads]
    states = [[None] * nch for _ in heads]
    for ch in range(nch):
        for hd in heads:
            stb = st[hd].astype(BF16)
            states[hd][ch] = stb
            st[hd] = st[hd] * hv[hd]["egl"][ch * c:ch * c + 1, :] - _dot(trans[hd][ch], stb) + drive[hd][ch]
    for hd in heads:
        state_ref[hd] = st[hd]
    for ch in range(nch):
        for hd in heads:
            lhs = jnp.concatenate([w_ref[hd, chunk(ch), :], hv[hd]["qg"][chunk(ch)]], axis=0).astype(BF16)
            ws = _dot(lhs, states[hd][ch])
            vnew_ref[hd, chunk(ch), :] = u_ref[hd, chunk(ch), :] - ws[:c]
            ointer_ref[hd, chunk(ch), :] = ws[c:]

    for (hd, sl), intra in zip(groups, intras):
        hl = slice(hd * HEAD_DIM, (hd + 1) * HEAD_DIM)
        zz = z_ref[0, sl, hl]
        o = ointer_ref[hd, sl, :] + _dot(intra, vnew_ref[hd, sl, :].astype(BF16))
        o_ref[0, sl, hl] = (_rms(o, onorm_ref[...]) * (zz * _sigmoid(zz))).astype(o_ref.dtype)


def _gdn(proj, ab, conv_w, alog_row, dtb_row, onorm_row):
    b, s, _ = proj.shape
    nhd = GDN_HEADS_PER_STEP
    wblk = nhd * HEAD_DIM
    hsteps = N_HEADS_HALF // nhd
    row = lambda col: pl.BlockSpec((1, GDN_ROWS, wblk), lambda bi, hi, si: (bi, si, col + hi))
    cw = lambda col: pl.BlockSpec((GDN_CONV, wblk), lambda bi, hi, si: (0, col + hi))
    vec = pl.BlockSpec((1, V7X_LANES), lambda bi, hi, si: (0, 0))
    per_head = lambda *shape: pltpu.VMEM((nhd,) + shape, F32)
    return pl.pallas_call(
        _gdn_kernel,
        out_shape=jax.ShapeDtypeStruct((b, s, GROUP_W), BF16),
        grid=(b, hsteps, s // GDN_ROWS),
        in_specs=[row(0), row(hsteps), row(2 * hsteps), row(3 * hsteps),
                  pl.BlockSpec((1, GDN_ROWS, V7X_LANES), lambda bi, hi, si: (bi, si, 0)),
                  cw(0), cw(hsteps), cw(2 * hsteps), vec, vec, vec],
        out_specs=pl.BlockSpec((1, GDN_ROWS, wblk), lambda bi, hi, si: (bi, si, hi)),
        scratch_shapes=[per_head(3, GDN_ROWS + GDN_HALO, HEAD_DIM),
                        per_head(HEAD_DIM, HEAD_DIM),
                        per_head(GDN_ROWS, HEAD_DIM), per_head(GDN_ROWS, HEAD_DIM),
                        per_head(GDN_ROWS, HEAD_DIM), per_head(GDN_ROWS, HEAD_DIM)],
        compiler_params=_cparams("parallel", "parallel", "arbitrary"),
        name="gated_delta",
    )(proj, proj, proj, proj, ab, conv_w, conv_w, conv_w, alog_row, dtb_row, onorm_row)


MOBA_MASK_BIG = 1e30
MOBA_TQ = 2 * MOBA_BLOCK
MOBA_SWEEP_BLOCKS = (4, 2)


def _moba_kernel(q_ref, k_ref, v_ref, o_ref, kaug_ref, vb_ref, qaug_ref, m_ref, l_ref, acc_ref, sbuf_ref,
                 *, seq):
    bs = MOBA_BLOCK
    tq = MOBA_TQ
    nb = seq // bs
    nl = V7X_LANES
    ncand = -(-nb // 8) * 8
    i = (tq // bs) * pl.program_id(2)
    exp2_scale = (HEAD_DIM ** -0.5) * 1.4426950408889634

    @pl.when(i == 0)
    def _():
        kf = k_ref[0]
        kaug_ref[:, :HEAD_DIM] = kf.astype(BF16)
        blk = lax.broadcasted_iota(jnp.int32, (seq, nl), 0) // bs
        col = lax.broadcasted_iota(jnp.int32, (seq, nl), 1)
        kaug_ref[:, HEAD_DIM:] = jnp.where(blk == col, MOBA_MASK_BIG, 0.0).astype(BF16)
        vb_ref[...] = v_ref[0].astype(BF16)
        rr = lax.broadcasted_iota(jnp.int32, (V7X_LANES, seq), 0)
        ss = lax.broadcasted_iota(jnp.int32, (V7X_LANES, seq), 1)
        onehot = jnp.where(ss // bs == rr, 1.0, 0.0).astype(BF16)
        tot = jnp.zeros((V7X_LANES, HEAD_DIM), F32)
        for piece in _split3(kf):
            tot = tot + _dot(onehot, piece)
        kmean = tot * (1.0 / bs)

        qaug_ref[:, :HEAD_DIM] = q_ref[0].astype(BF16)
        cand_idx = lax.broadcasted_iota(jnp.int32, (ncand, tq), 0)
        cand_f = cand_idx.astype(F32)
        for tile in range(seq // tq):
            rows = slice(tile * tq, (tile + 1) * tq)
            own = tile * (tq // bs) + lax.broadcasted_iota(jnp.int32, (ncand, tq), 1) // bs
            gate = jnp.where(cand_idx < own, _dot3_nt(kmean, q_ref[0, rows, :])[:ncand], NEG_INF)
            picked_t = jnp.zeros((ncand, tq), F32)
            for _ in range(MOBA_TOPK):
                mx = jnp.max(gate, axis=0, keepdims=True)
                first = jnp.min(jnp.where((gate == mx) & (mx > NEG_INF), cand_f, float(ncand)),
                                axis=0, keepdims=True)
                hit = cand_f == first
                picked_t = jnp.where(hit, 1.0, picked_t)
                gate = jnp.where(hit, NEG_INF, gate)
            allowed_t = jnp.where(cand_idx == own, 1.0, picked_t)
            allowed = jnp.concatenate([allowed_t, jnp.zeros((nl - ncand, tq), F32)], axis=0).T
            qaug_ref[rows, HEAD_DIM:] = (allowed - 1.0).astype(BF16)

    q_aug = qaug_ref[pl.ds(pl.multiple_of(i * bs, tq), tq), :]

    def raw_scores(first, nblocks):
        start = pl.multiple_of(first * bs, bs)
        return _dot_nt(q_aug, kaug_ref[pl.ds(start, nblocks * bs), :])

    def lane_groups(x):
        return [x[:, g * nl:(g + 1) * nl] for g in range(x.shape[1] // nl)]

    def sweep_past(step):
        wide = MOBA_SWEEP_BLOCKS[0]

        def wide_body(t, c):
            step(t * wide, wide)
            return c

        lax.fori_loop(0, i // wide, wide_body, 0)
        for w in MOBA_SWEEP_BLOCKS[1:]:
            @pl.when((i // w) % 2 == 1)
            def _(w=w):
                step((i // (2 * w)) * (2 * w), w)

    own0 = pl.multiple_of(i * bs, tq)
    row2 = lax.broadcasted_iota(jnp.int32, (tq, tq), 0)
    col2 = lax.broadcasted_iota(jnp.int32, (tq, tq), 1)
    s_own = jnp.where(col2 <= row2, _dot_nt(q_aug, kaug_ref[pl.ds(own0, tq), :]), -MOBA_MASK_BIG)

    m_ref[...] = functools.reduce(jnp.maximum, lane_groups(s_own))

    def max_step(first, nblocks):
        sc = raw_scores(first, nblocks)
        sbuf_ref[:, pl.ds(pl.multiple_of(first * bs, bs), nblocks * bs)] = sc
        m_ref[...] = functools.reduce(jnp.maximum, [m_ref[...]] + lane_groups(sc))

    sweep_past(max_step)
    m_ref[...] = jnp.broadcast_to(jnp.max(m_ref[...], axis=-1, keepdims=True), (tq, nl))

    def accumulate(sc, first_row, init):
        mrow = m_ref[...]
        ps = [jnp.exp2((g - mrow) * exp2_scale) for g in lane_groups(sc)]
        psum = functools.reduce(lambda a, b_: a + b_, ps)
        pv = _dot(jnp.concatenate(ps, axis=1).astype(BF16), vb_ref[pl.ds(first_row, sc.shape[1]), :])
        if init:
            l_ref[...] = psum
            acc_ref[...] = pv
        else:
            l_ref[...] += psum
            acc_ref[...] += pv

    accumulate(s_own, own0, init=True)

    def acc_step(first, nblocks):
        start = pl.multiple_of(first * bs, bs)
        accumulate(sbuf_ref[:, pl.ds(start, nblocks * bs)], start, init=False)

    sweep_past(acc_step)
    o_ref[0] = (acc_ref[...] / jnp.sum(l_ref[...], axis=-1, keepdims=True)).astype(o_ref.dtype)


def _moba(proj, q_col, k_col, v_col):
    b, s, _ = proj.shape
    h = N_HEADS_HALF
    tq = MOBA_TQ
    assert s % (MOBA_BLOCK * MOBA_SWEEP_BLOCKS[0]) == 0 and s // MOBA_BLOCK <= V7X_LANES
    return pl.pallas_call(
        functools.partial(_moba_kernel, seq=s),
        out_shape=jax.ShapeDtypeStruct((b, s, GROUP_W), BF16),
        grid=(b, h, s // tq),
        in_specs=[pl.BlockSpec((1, s, HEAD_DIM), lambda bi, hi, i: (bi, 0, q_col + hi)),
                  pl.BlockSpec((1, s, HEAD_DIM), lambda bi, hi, i: (bi, 0, k_col + hi)),
                  pl.BlockSpec((1, s, HEAD_DIM), lambda bi, hi, i: (bi, 0, v_col + hi))],
        out_specs=pl.BlockSpec((1, tq, HEAD_DIM), lambda bi, hi, i: (bi, i, hi)),
        scratch_shapes=[pltpu.VMEM((s, HEAD_DIM + V7X_LANES), BF16), pltpu.VMEM((s, HEAD_DIM), BF16),
                        pltpu.VMEM((s, HEAD_DIM + V7X_LANES), BF16),
                        pltpu.VMEM((tq, V7X_LANES), F32), pltpu.VMEM((tq, V7X_LANES), F32),
                        pltpu.VMEM((tq, HEAD_DIM), F32), pltpu.VMEM((tq, s), F32)],
        compiler_params=_cparams("parallel", "parallel", "arbitrary"),
        name="moba_attn",
    )(proj, proj, proj)


DIL_COPY_ROWS = 256
DIL_BLOCKS_PER_STEP = 8


def _dil_kernel(q_ref, k_ref, v_ref, o_ref, qd_ref, kd_ref, vd_ref, tq_ref, tk_ref, tv_ref, obuf_ref, lbuf_ref,
                *, seq):
    span = DIL_SPAN
    scale = HEAD_DIM ** -0.5
    kd_ref[0:span, :] = jnp.zeros((span, HEAD_DIM), BF16)
    vd_ref[0:span, :] = jnp.zeros((span, HEAD_DIM), BF16)
    qi = lax.broadcasted_iota(jnp.int32, (span, 2 * span), 0)
    ki = lax.broadcasted_iota(jnp.int32, (span, 2 * span), 1)
    in_window = (ki >= qi) & (ki <= qi + span)

    for g, rate in enumerate(DIL_RATES):
        n = seq // rate
        copies_per_res = n // DIL_COPY_ROWS
        nblk = n // span

        prev = DIL_RATES[g - 1] if g > 0 else 1
        staged = prev > 1 and rate % prev == 0
        keep_f32 = rate > 1 and g + 1 < len(DIL_RATES) and DIL_RATES[g + 1] % rate == 0
        step_rows = rate // prev if staged else rate

        def copy(c, cc, rate=rate, copies_per_res=copies_per_res, prev=prev, staged=staged,
                 keep_f32=keep_f32, step_rows=step_rows):
            rho = c // copies_per_res
            within = (c % copies_per_res) * (DIL_COPY_ROWS * step_rows)
            if staged:
                src = (rho % prev) * (seq // prev) + rho // prev + within
            else:
                src = rho + within
            dst = pl.multiple_of(c * DIL_COPY_ROWS, DIL_COPY_ROWS)
            idx = pl.ds(src, DIL_COPY_ROWS) if step_rows == 1 else pl.ds(src, DIL_COPY_ROWS, stride=step_rows)
            for x_ref, t_ref, d_ref, off in ((q_ref, tq_ref, qd_ref, 0), (k_ref, tk_ref, kd_ref, span),
                                             (v_ref, tv_ref, vd_ref, span)):
                x = t_ref[idx, :] if staged else x_ref[0, idx, :]
                if keep_f32:
                    t_ref[pl.ds(dst, DIL_COPY_ROWS), :] = x
                d_ref[pl.ds(off + dst, DIL_COPY_ROWS), :] = x.astype(BF16)
            return cc

        lax.fori_loop(0, seq // DIL_COPY_ROWS, copy, 0)

        def blocks(step, cc, g=g, rate=rate, nblk=nblk):
            blks = [step * DIL_BLOCKS_PER_STEP + sub for sub in range(DIL_BLOCKS_PER_STEP)]
            r0s = [pl.multiple_of(blk * span, span) for blk in blks]
            raw = [_dot_nt(qd_ref[pl.ds(r0, span), :], kd_ref[pl.ds(r0, 2 * span), :]) for r0 in r0s]
            scs = [jnp.where(in_window & ((blk % nblk > 0) | (ki >= span)), s * scale, NEG_INF)
                   for blk, s in zip(blks, raw)]
            ms = [jnp.max(sc, axis=-1, keepdims=True) for sc in scs]
            pr = [jnp.exp(sc - m) for sc, m in zip(scs, ms)]
            dens = [jnp.sum(p, axis=-1, keepdims=True) for p in pr]
            pvs = [_dot(p.astype(BF16), vd_ref[pl.ds(r0, 2 * span), :]) for p, r0 in zip(pr, r0s)]
            for blk, pv, m, den in zip(blks, pvs, ms, dens):
                dst = blk // nblk + (blk % nblk) * (span * rate)
                idx = pl.ds(dst, span) if rate == 1 else pl.ds(dst, span, stride=rate)
                obuf_ref[g, idx, :] = pv / den
                lbuf_ref[g, idx, :] = jnp.broadcast_to(m + jnp.log(den), (span, HEAD_DIM))
            return cc

        lax.fori_loop(0, seq // span // DIL_BLOCKS_PER_STEP, blocks, 0)

    def merge(t, c):
        r0 = pl.multiple_of(t * DIL_COPY_ROWS, DIL_COPY_ROWS)
        sl = pl.ds(r0, DIL_COPY_ROWS)
        ls = [lbuf_ref[g, sl, :] for g in range(len(DIL_RATES))]
        mx = functools.reduce(jnp.maximum, ls)
        es = [jnp.exp(l - mx) for l in ls]
        num = functools.reduce(lambda a, b_: a + b_, [e * obuf_ref[g, sl, :] for g, e in enumerate(es)])
        den = functools.reduce(lambda a, b_: a + b_, es)
        o_ref[0, sl, :] = (num / den).astype(o_ref.dtype)
        return c

    lax.fori_loop(0, seq // DIL_COPY_ROWS, merge, 0)


def _dilated(proj, q_col, k_col, v_col):
    b, s, _ = proj.shape
    h = N_HEADS_HALF
    ng = len(DIL_RATES)
    full = lambda col: pl.BlockSpec((1, s, HEAD_DIM), lambda bi, hi: (bi, 0, col + hi))
    return pl.pallas_call(
        functools.partial(_dil_kernel, seq=s),
        out_shape=jax.ShapeDtypeStruct((b, s, GROUP_W), BF16),
        grid=(b, h),
        in_specs=[full(q_col), full(k_col), full(v_col)],
        out_specs=pl.BlockSpec((1, s, HEAD_DIM), lambda bi, hi: (bi, 0, hi)),
        scratch_shapes=[pltpu.VMEM((s, HEAD_DIM), BF16),
                        pltpu.VMEM((s + DIL_SPAN, HEAD_DIM), BF16),
                        pltpu.VMEM((s + DIL_SPAN, HEAD_DIM), BF16),
                        pltpu.VMEM((s, HEAD_DIM), F32), pltpu.VMEM((s, HEAD_DIM), F32),
                        pltpu.VMEM((s, HEAD_DIM), F32),
                        pltpu.VMEM((ng, s, HEAD_DIM), F32), pltpu.VMEM((ng, s, HEAD_DIM), F32)],
        compiler_params=_cparams("parallel", "parallel"),
        name="dilated_attn",
    )(proj, proj, proj)


def _residual_epilogue(y, h_ref, gpost_ref, gnext_ref, h_out_ref, u_out_ref):
    hn = h_ref[...] + _rms(y, gpost_ref[...])
    h_out_ref[...] = hn
    if u_out_ref is not None:
        u_out_ref[...] = _rms(hn, gnext_ref[...]).astype(u_out_ref.dtype)


def _outproj_kernel(a_ref, b_ref, wa_ref, wb_ref, h_ref, gpost_ref, gnext_ref, h_out_ref, u_out_ref):
    y = _dot(a_ref[...], wa_ref[...]) + _dot(b_ref[...], wb_ref[...])
    _residual_epilogue(y, h_ref, gpost_ref, gnext_ref, h_out_ref, u_out_ref)


def _outproj(a, b, wa, wb, h, g_post, g_next, tm=512):
    m, d = h.shape
    ka = a.shape[1]
    kb = b.shape[1]
    rowblk = lambda w: pl.BlockSpec((tm, w), lambda i: (i, 0))
    const = lambda r, c: pl.BlockSpec((r, c), lambda i: (0, 0))
    return pl.pallas_call(
        _outproj_kernel,
        out_shape=(jax.ShapeDtypeStruct((m, d), F32), jax.ShapeDtypeStruct((m, d), BF16)),
        grid=(m // tm,),
        in_specs=[rowblk(ka), rowblk(kb), const(ka, d), const(kb, d), rowblk(d), const(1, d), const(1, d)],
        out_specs=(rowblk(d), rowblk(d)),
        compiler_params=_cparams("parallel"),
        name="outproj_residual",
    )(a, b, wa, wb, h, g_post.reshape(1, d), g_next.reshape(1, d))


def _ffn_kernel(u_ref, wg_ref, wu_ref, wd_ref, h_ref, gpost_ref, gnext_ref, *rest, emit_next):
    if emit_next:
        h_out_ref, u_out_ref, acc_ref = rest
    else:
        h_out_ref, acc_ref = rest
        u_out_ref = None
    f = pl.program_id(1)

    @pl.when(f == 0)
    def _():
        acc_ref[...] = jnp.zeros_like(acc_ref)

    u = u_ref[...]
    gate = _dot(u, wg_ref[...])
    up = _dot(u, wu_ref[...])
    act = (gate * _sigmoid(gate) * up).astype(BF16)
    acc_ref[...] += _dot(act, wd_ref[...])

    @pl.when(f == pl.num_programs(1) - 1)
    def _():
        _residual_epilogue(acc_ref[...], h_ref, gpost_ref, gnext_ref, h_out_ref, u_out_ref)


def _ffn(u, wg, wu, wd, h, g_post, g_next, emit_next, tm=512, tf=512):
    m, d = h.shape
    ff = wg.shape[1]
    rowblk = pl.BlockSpec((tm, d), lambda i, f: (i, 0))
    const = pl.BlockSpec((1, d), lambda i, f: (0, 0))
    out_shape = [jax.ShapeDtypeStruct((m, d), F32)]
    out_specs = [rowblk]
    if emit_next:
        out_shape.append(jax.ShapeDtypeStruct((m, d), BF16))
        out_specs.append(rowblk)
    res = pl.pallas_call(
        functools.partial(_ffn_kernel, emit_next=emit_next),
        out_shape=tuple(out_shape),
        grid=(m // tm, ff // tf),
        in_specs=[rowblk,
                  pl.BlockSpec((d, tf), lambda i, f: (0, f)),
                  pl.BlockSpec((d, tf), lambda i, f: (0, f)),
                  pl.BlockSpec((tf, d), lambda i, f: (f, 0)),
                  rowblk, const, const],
        out_specs=tuple(out_specs),
        scratch_shapes=[pltpu.VMEM((tm, d), F32)],
        compiler_params=_cparams("parallel", "arbitrary"),
        name="swiglu_residual",
    )(u, wg, wu, wd, h, g_post.reshape(1, d), g_next.reshape(1, d))
    return res if emit_next else (res[0], None)


def _pad_lanes(vec, offset=0):
    out = jnp.zeros((1, V7X_LANES), F32)
    return lax.dynamic_update_slice(out, vec.reshape(1, -1).astype(F32), (0, offset))


def kernel(x, mix_norm_pre, mix_norm_post, ffn_norm_pre, ffn_norm_post, ev_w_in, ev_conv_w, ev_a_log,
           ev_dt_bias, ev_onorm, ev_w_out, od_w_in, od_w_out, ffn_w_gate, ffn_w_up, ffn_w_down):
    b, s, d = x.shape
    m = b * s
    gw = GROUP_W
    nh = N_HEADS_HALF
    h = x.reshape(m, d)

    u = _prenorm(h, mix_norm_pre[0])
    w_in = ev_w_in[0]
    n_sb = 3 * gw
    n_gdn = 4 * gw
    proj_sb = _matmul(u, w_in[:, :n_sb].astype(BF16), out_dtype=BF16).reshape(b, s, n_sb)
    proj_gdn = _matmul(u, w_in[:, n_sb:n_sb + n_gdn].astype(BF16)).reshape(b, s, n_gdn)
    w_ab = jnp.pad(w_in[:, n_sb + n_gdn:], ((0, 0), (0, V7X_LANES - 2 * nh))).astype(BF16)
    proj_ab = _matmul(u, w_ab).reshape(b, s, V7X_LANES)
    o_sb = _sb_attention(proj_sb, 0, nh, 2 * nh)
    o_gdn = _gdn(proj_gdn, proj_ab, ev_conv_w[0], _pad_lanes(ev_a_log[0]), _pad_lanes(ev_dt_bias[0]),
                 ev_onorm[0].reshape(1, HEAD_DIM))
    w_out = ev_w_out[0].astype(BF16)
    h, u = _outproj(o_sb.reshape(m, gw), o_gdn.reshape(m, gw), w_out[:gw], w_out[gw:], h,
                    mix_norm_post[0], ffn_norm_pre[0])
    h, u = _ffn(u, ffn_w_gate[0].astype(BF16), ffn_w_up[0].astype(BF16), ffn_w_down[0].astype(BF16), h,
                ffn_norm_post[0], mix_norm_pre[1], emit_next=True)

    proj_od = _matmul(u, od_w_in[0].astype(BF16)).reshape(b, s, 6 * gw)
    o_moba = _moba(proj_od, 0, nh, 2 * nh)
    o_dil = _dilated(proj_od, 3 * nh, 4 * nh, 5 * nh)
    w_out = od_w_out[0].astype(BF16)
    h, u = _outproj(o_moba.reshape(m, gw), o_dil.reshape(m, gw), w_out[:gw], w_out[gw:], h,
                    mix_norm_post[1], ffn_norm_pre[1])
    h, _ = _ffn(u, ffn_w_gate[1].astype(BF16), ffn_w_up[1].astype(BF16), ffn_w_down[1].astype(BF16), h,
                ffn_norm_post[1], ffn_norm_post[1], emit_next=False)
    return h.reshape(b, s, d)
```

```python
import functools

import jax
import jax.numpy as jnp
from jax import lax
from jax.experimental import pallas as pl
from jax.experimental.pallas import tpu as pltpu

F32 = jnp.float32
BF16 = jnp.bfloat16

HEAD_DIM = 128
N_HEADS_HALF = 8
GROUP_W = N_HEADS_HALF * HEAD_DIM
NORM_EPS = 1e-6
GDN_CHUNK = 64
GDN_CONV = 4
MOBA_BLOCK = 256
MOBA_TOPK = 3
DIL_SPAN = 128
DIL_RATES = (1, 4, 16)

V7X_LANES = 128
V7X_VMEM_LIMIT_BYTES = 56 * 1024 * 1024

PRENORM_TM = 512
PROJ_TM, PROJ_TN = 2048, 512
OUTPROJ_TM = 512
FFN_TM, FFN_TF = 512, 512

NEG_INF = float("-inf")


def _cparams(*sem):
    return pltpu.CompilerParams(dimension_semantics=sem, vmem_limit_bytes=V7X_VMEM_LIMIT_BYTES)


def _dot(a, b):
    return jnp.dot(a, b, preferred_element_type=F32)


def _dot_nt(a, b):
    return lax.dot_general(a, b, (((1,), (1,)), ((), ())), preferred_element_type=F32)


def _dot_tn(a, b):
    return lax.dot_general(a, b, (((0,), (0,)), ((), ())), preferred_element_type=F32)


def _split2(x):
    hi = x.astype(BF16)
    lo = (x - hi.astype(F32)).astype(BF16)
    return hi, lo


def _split3(x):
    p1 = x.astype(BF16)
    r1 = x - p1.astype(F32)
    p2 = r1.astype(BF16)
    p3 = (r1 - p2.astype(F32)).astype(BF16)
    return p1, p2, p3


def _dot3_nt(a, b):
    a1, a2 = _split2(a)
    b1, b2 = _split2(b)
    return _dot_nt(a1, b1) + _dot_nt(a1, b2) + _dot_nt(a2, b1)


def _softplus(x):
    return jnp.maximum(x, 0.0) + jnp.log1p(jnp.exp(-jnp.abs(x)))


def _sigmoid(x):
    return 1.0 / (1.0 + jnp.exp(-x))


def _rms(x, gain):
    return x * lax.rsqrt(jnp.mean(x * x, axis=-1, keepdims=True) + NORM_EPS) * gain


def _prenorm_kernel(x_ref, g_ref, o_ref):
    o_ref[...] = _rms(x_ref[...], g_ref[...]).astype(o_ref.dtype)


def _prenorm(x2d, gain, tm=PRENORM_TM):
    m, d = x2d.shape
    return pl.pallas_call(
        _prenorm_kernel,
        out_shape=jax.ShapeDtypeStruct((m, d), BF16),
        grid=(m // tm,),
        in_specs=[pl.BlockSpec((tm, d), lambda i: (i, 0)),
                  pl.BlockSpec((1, d), lambda i: (0, 0))],
        out_specs=pl.BlockSpec((tm, d), lambda i: (i, 0)),
        compiler_params=_cparams("parallel"),
        name="prenorm",
    )(x2d, gain.reshape(1, d))


def _matmul_kernel(x_ref, w_ref, o_ref):
    o_ref[...] = _dot(x_ref[...], w_ref[...]).astype(o_ref.dtype)


def _matmul(x, w, out_dtype=F32, tm=PROJ_TM, tn=PROJ_TN):
    m, k = x.shape
    n = w.shape[1]
    tn = min(tn, n)
    return pl.pallas_call(
        _matmul_kernel,
        out_shape=jax.ShapeDtypeStruct((m, n), out_dtype),
        grid=(m // tm, n // tn),
        in_specs=[pl.BlockSpec((tm, k), lambda i, j: (i, 0)),
                  pl.BlockSpec((k, tn), lambda i, j: (0, j))],
        out_specs=pl.BlockSpec((tm, tn), lambda i, j: (i, j)),
        compiler_params=_cparams("parallel", "arbitrary"),
        name="proj_matmul",
    )(x, w)


SB_TQ = 512
SB_TK = 512
SB_PAIR = 2 * V7X_LANES


def _sb_sweep_step(q, kb_ref, vb_ref, acc_ref, carry_ref, suffix_op, start, qpos, masked):
    scale = HEAD_DIM ** -0.5
    log2e = 1.4426950408889634
    nl = V7X_LANES
    pairs = range(SB_TK // SB_PAIR - 1, -1, -1)
    p0s = [pl.multiple_of(start + pair * SB_PAIR, SB_PAIR) for pair in pairs]
    raws = [_dot_nt(q, kb_ref[pl.ds(p0, SB_PAIR), :]) for p0 in p0s]
    log_sig, pasts, parts = [], [], []
    for p0, raw in zip(p0s, raws):
        z = raw * scale
        sp = jnp.maximum(z, 0.0) + jnp.log(1.0 + jnp.exp2(jnp.abs(raw) * (-scale * log2e)))
        log_sig.append(z - sp)
        if masked:
            past = (p0 + lax.broadcasted_iota(jnp.int32, (SB_TQ, SB_PAIR), 1)) < qpos
            sp = jnp.where(past, sp, 0.0)
            pasts.append(past)
        hi, lo = _split2(sp)
        parts.append((jnp.concatenate([hi[:, nl:], lo[:, nl:]], axis=1),
                      jnp.concatenate([hi[:, :nl], lo[:, :nl]], axis=1)))
    sums = [(_dot(near, suffix_op), _dot(far, suffix_op)) for near, far in parts]
    carry = carry_ref[...]
    betweens = []
    for sr_near, sr_far in sums:
        c_far = carry + sr_near[:, nl:]
        betweens.append(jnp.concatenate([sr_far[:, :nl] + c_far, sr_near[:, :nl] + carry], axis=1))
        carry = c_far + sr_far[:, nl:]
    carry_ref[...] = carry
    ws = []
    for idx, (ls, between) in enumerate(zip(log_sig, betweens)):
        w = jnp.exp2((ls - between) * log2e)
        if masked:
            w = jnp.where(pasts[idx], w, 0.0)
        ws.append(w.astype(BF16))
    acc_ref[...] += sum(_dot(w, vb_ref[pl.ds(p0, SB_PAIR), :]) for w, p0 in zip(ws, p0s))


def _sb_kernel(q_ref, k_ref, v_ref, o_ref, kb_ref, vb_ref, acc_ref, carry_ref):
    i = pl.program_id(2)

    @pl.when(i == 0)
    def _():
        kb_ref[...] = k_ref[0].astype(BF16)
        vb_ref[...] = v_ref[0].astype(BF16)

    q = q_ref[0].astype(BF16)
    acc_ref[...] = jnp.zeros_like(acc_ref)
    carry_ref[...] = jnp.zeros_like(carry_ref)

    kk = lax.broadcasted_iota(jnp.int32, (SB_PAIR, SB_PAIR), 0) % V7X_LANES
    cc = lax.broadcasted_iota(jnp.int32, (SB_PAIR, SB_PAIR), 1)
    suffix_op = jnp.where((cc >= V7X_LANES) | (kk > cc), 1.0, 0.0).astype(BF16)

    qpos = i * SB_TQ + lax.broadcasted_iota(jnp.int32, (SB_TQ, SB_PAIR), 0)
    diag = (i * SB_TQ) // SB_TK
    _sb_sweep_step(q, kb_ref, vb_ref, acc_ref, carry_ref, suffix_op, diag * SB_TK, qpos, masked=True)

    def body(t, c):
        _sb_sweep_step(q, kb_ref, vb_ref, acc_ref, carry_ref, suffix_op, (diag - t) * SB_TK, None, masked=False)
        return c

    lax.fori_loop(1, diag + 1, body, 0)
    o_ref[0] = acc_ref[...].astype(o_ref.dtype)


def _sb_attention(proj, q_col, k_col, v_col):
    b, s, _ = proj.shape
    h = N_HEADS_HALF
    return pl.pallas_call(
        _sb_kernel,
        out_shape=jax.ShapeDtypeStruct((b, s, GROUP_W), BF16),
        grid=(b, h, s // SB_TQ),
        in_specs=[pl.BlockSpec((1, SB_TQ, HEAD_DIM), lambda bi, hi, i: (bi, i, q_col + hi)),
                  pl.BlockSpec((1, s, HEAD_DIM), lambda bi, hi, i: (bi, 0, k_col + hi)),
                  pl.BlockSpec((1, s, HEAD_DIM), lambda bi, hi, i: (bi, 0, v_col + hi))],
        out_specs=pl.BlockSpec((1, SB_TQ, HEAD_DIM), lambda bi, hi, i: (bi, i, hi)),
        scratch_shapes=[pltpu.VMEM((s, HEAD_DIM), BF16), pltpu.VMEM((s, HEAD_DIM), BF16),
                        pltpu.VMEM((SB_TQ, HEAD_DIM), F32), pltpu.VMEM((SB_TQ, HEAD_DIM), F32)],
        compiler_params=_cparams("parallel", "parallel", "arbitrary"),
        name="stickbreak_attn",
    )(proj, proj, proj)


GDN_ROWS = 512
GDN_GROUP = 256
GDN_HALO = 8
GDN_HEADS_PER_STEP = 8


def _gdn_prep(hd, xq_ref, xk_ref, xv_ref, ab_ref, wq_ref, wk_ref, wv_ref, alog_ref, dtb_ref, xbuf_ref):
    h = pl.program_id(1) * GDN_HEADS_PER_STEP + hd
    hl = slice(hd * HEAD_DIM, (hd + 1) * HEAD_DIM)
    rows = GDN_ROWS
    c = GDN_CHUNK

    conv = []
    for idx, (x_ref, cw_ref) in enumerate(((xq_ref, wq_ref), (xk_ref, wk_ref), (xv_ref, wv_ref))):
        xbuf_ref[hd, idx, GDN_HALO:GDN_HALO + rows, :] = x_ref[0, :, hl]
        y = jnp.zeros((rows, HEAD_DIM), F32)
        for tap in range(GDN_CONV):
            shift = GDN_CONV - 1 - tap
            y = y + cw_ref[tap:tap + 1, hl] * xbuf_ref[hd, idx, GDN_HALO - shift:GDN_HALO - shift + rows, :]
        xbuf_ref[hd, idx, 0:GDN_HALO, :] = xbuf_ref[hd, idx, rows:rows + GDN_HALO, :]
        conv.append(y * _sigmoid(y))
    yq, yk, yv = conv
    q = yq * lax.rsqrt(jnp.sum(yq * yq, axis=-1, keepdims=True) + NORM_EPS) * (HEAD_DIM ** -0.5)
    k = yk * lax.rsqrt(jnp.sum(yk * yk, axis=-1, keepdims=True) + NORM_EPS)
    v = yv

    ab = ab_ref[0]
    g_all = -jnp.exp(alog_ref[...]) * _softplus(ab + dtb_ref[...])
    beta_all = _sigmoid(ab)
    ab_lane = lax.broadcasted_iota(jnp.int32, (rows, V7X_LANES), 1)
    g_col = jnp.sum(jnp.where(ab_lane == h, g_all, 0.0), axis=-1, keepdims=True)
    beta = jnp.sum(jnp.where(ab_lane == h + N_HEADS_HALF, beta_all, 0.0), axis=-1, keepdims=True)
    g_rep = jnp.broadcast_to(g_col, (rows, HEAD_DIM))
    ri = lax.broadcasted_iota(jnp.int32, (GDN_GROUP, GDN_GROUP), 0)
    ci = lax.broadcasted_iota(jnp.int32, (GDN_GROUP, GDN_GROUP), 1)
    cum_op = jnp.where((ri // c == ci // c) & (ci <= ri), 1.0, 0.0).astype(BF16)
    gc = jnp.concatenate(
        [sum(_dot(cum_op, piece) for piece in _split3(g_rep[r0:r0 + GDN_GROUP]))
         for r0 in range(0, rows, GDN_GROUP)], axis=0)

    gc3 = gc.reshape(rows // c, c, HEAD_DIM)
    gl = jnp.broadcast_to(gc3[:, c - 1:c, :], (rows // c, c, HEAD_DIM)).reshape(rows, HEAD_DIM)
    eg = jnp.exp(gc)
    egl = jnp.exp(gl)
    kd = k * jnp.exp(gl - gc)
    qg = q * eg
    kb = k * beta
    rhs_uw = jnp.concatenate([v * beta, kb * eg], axis=1)
    return dict(q=q, k=k, kb=kb, kd=kd, qg=qg, gc=gc, egl=egl, rhs_uw=rhs_uw)


def _gdn_kernel(xq_ref, xk_ref, xv_ref, z_ref, ab_ref, wq_ref, wk_ref, wv_ref, alog_ref, dtb_ref,
                onorm_ref, o_ref, xbuf_ref, state_ref, u_ref, w_ref, vnew_ref, ointer_ref):
    rows = GDN_ROWS
    c = GDN_CHUNK
    nch = rows // c
    heads = range(GDN_HEADS_PER_STEP)
    groups = [(hd, slice(r0, r0 + GDN_GROUP)) for hd in heads for r0 in range(0, rows, GDN_GROUP)]

    @pl.when(pl.program_id(2) == 0)
    def _():
        state_ref[...] = jnp.zeros_like(state_ref)
        xbuf_ref[:, :, 0:GDN_HALO, :] = jnp.zeros((GDN_HEADS_PER_STEP, 3, GDN_HALO, HEAD_DIM), F32)

    hv = [_gdn_prep(hd, xq_ref, xk_ref, xv_ref, ab_ref, wq_ref, wk_ref, wv_ref, alog_ref, dtb_ref, xbuf_ref)
          for hd in heads]

    lane = lax.broadcasted_iota(jnp.int32, (GDN_GROUP, V7X_LANES), 1)
    gi = lax.broadcasted_iota(jnp.int32, (GDN_GROUP, GDN_GROUP), 0)
    gj = lax.broadcasted_iota(jnp.int32, (GDN_GROUP, GDN_GROUP), 1)
    same = (gi // c) == (gj // c)
    lower_incl = same & (gj <= gi)
    lower_strict = same & (gj < gi)
    eye = jnp.where(gi == gj, 1.0, 0.0)

    ps, xs, intras = [], [], []
    for hd, sl in groups:
        p1, p2, p3 = (piece.astype(F32) for piece in _split3(hv[hd]["gc"][sl]))
        lhs = jnp.where(lane == 0, p1, jnp.where(lane == 1, p2, jnp.where(lane == 2, p3,
              jnp.where(lane < 6, 1.0, 0.0))))
        rhs = jnp.where(lane < 3, 1.0, jnp.where(lane == 3, -p1, jnp.where(lane == 4, -p2,
              jnp.where(lane == 5, -p3, 0.0))))
        diff = _dot_nt(lhs.astype(BF16), rhs.astype(BF16))
        decay = jnp.exp(jnp.where(lower_incl, diff, 0.0))
        kgb = hv[hd]["k"][sl].astype(BF16)
        a_mat = jnp.where(lower_strict, _dot_nt(hv[hd]["kb"][sl].astype(BF16), kgb) * decay, 0.0)
        intras.append(jnp.where(lower_incl, _dot_nt(hv[hd]["q"][sl].astype(BF16), kgb) * decay, 0.0).astype(BF16))
        ps.append(-a_mat)
        xs.append(eye - a_mat)

    for _ in range(5):
        pbs = [p.astype(BF16) for p in ps]
        ps = [_dot(pb, pb) for pb in pbs]
        xs = [x + _dot(x.astype(BF16), p.astype(BF16)) for x, p in zip(xs, ps)]
    for (hd, sl), x in zip(groups, xs):
        uw = _dot(x.astype(BF16), hv[hd]["rhs_uw"][sl].astype(BF16))
        u_ref[hd, sl, :] = uw[:, :HEAD_DIM]
        w_ref[hd, sl, :] = uw[:, HEAD_DIM:]

    chunk = lambda ch: slice(ch * c, (ch + 1) * c)
    trans = [[None] * nch for _ in heads]
    drive = [[None] * nch for _ in heads]
    for ch in range(nch):
        for hd in heads:
            kdb = hv[hd]["kd"][chunk(ch)].astype(BF16)
            trans[hd][ch] = _dot_tn(kdb, w_ref[hd, chunk(ch), :].astype(BF16)).astype(BF16)
            drive[hd][ch] = _dot_tn(kdb, u_ref[hd, chunk(ch), :].astype(BF16))
    st = [state_ref[hd] for hd in heads]
    states = [[None] * nch for _ in heads]
    for ch in range(nch):
        for hd in heads:
            stb = st[hd].astype(BF16)
            states[hd][ch] = stb
            st[hd] = st[hd] * hv[hd]["egl"][ch * c:ch * c + 1, :] - _dot(trans[hd][ch], stb) + drive[hd][ch]
    for hd in heads:
        state_ref[hd] = st[hd]
    for ch in range(nch):
        for hd in heads:
            lhs = jnp.concatenate([w_ref[hd, chunk(ch), :], hv[hd]["qg"][chunk(ch)]], axis=0).astype(BF16)
            ws = _dot(lhs, states[hd][ch])
            vnew_ref[hd, chunk(ch), :] = u_ref[hd, chunk(ch), :] - ws[:c]
            ointer_ref[hd, chunk(ch), :] = ws[c:]

    for (hd, sl), intra in zip(groups, intras):
        hl = slice(hd * HEAD_DIM, (hd + 1) * HEAD_DIM)
        zz = z_ref[0, sl, hl]
        o = ointer_ref[hd, sl, :] + _dot(intra, vnew_ref[hd, sl, :].astype(BF16))
        o_ref[0, sl, hl] = (_rms(o, onorm_ref[...]) * (zz * _sigmoid(zz))).astype(o_ref.dtype)


def _gdn(proj, ab, conv_w, alog_row, dtb_row, onorm_row):
    b, s, _ = proj.shape
    nhd = GDN_HEADS_PER_STEP
    wblk = nhd * HEAD_DIM
    hsteps = N_HEADS_HALF // nhd
    row = lambda col: pl.BlockSpec((1, GDN_ROWS, wblk), lambda bi, hi, si: (bi, si, col + hi))
    cw = lambda col: pl.BlockSpec((GDN_CONV, wblk), lambda bi, hi, si: (0, col + hi))
    vec = pl.BlockSpec((1, V7X_LANES), lambda bi, hi, si: (0, 0))
    per_head = lambda *shape: pltpu.VMEM((nhd,) + shape, F32)
    return pl.pallas_call(
        _gdn_kernel,
        out_shape=jax.ShapeDtypeStruct((b, s, GROUP_W), BF16),
        grid=(b, hsteps, s // GDN_ROWS),
        in_specs=[row(0), row(hsteps), row(2 * hsteps), row(3 * hsteps),
                  pl.BlockSpec((1, GDN_ROWS, V7X_LANES), lambda bi, hi, si: (bi, si, 0)),
                  cw(0), cw(hsteps), cw(2 * hsteps), vec, vec, vec],
        out_specs=pl.BlockSpec((1, GDN_ROWS, wblk), lambda bi, hi, si: (bi, si, hi)),
        scratch_shapes=[per_head(3, GDN_ROWS + GDN_HALO, HEAD_DIM),
                        per_head(HEAD_DIM, HEAD_DIM),
                        per_head(GDN_ROWS, HEAD_DIM), per_head(GDN_ROWS, HEAD_DIM),
                        per_head(GDN_ROWS, HEAD_DIM), per_head(GDN_ROWS, HEAD_DIM)],
        compiler_params=_cparams("parallel", "parallel", "arbitrary"),
        name="gated_delta",
    )(proj, proj, proj, proj, ab, conv_w, conv_w, conv_w, alog_row, dtb_row, onorm_row)


MOBA_MASK_BIG = 1e30
MOBA_TQ = 2 * MOBA_BLOCK
MOBA_SWEEP_BLOCKS = (4, 2)


def _moba_kernel(q_ref, k_ref, v_ref, o_ref, kaug_ref, vb_ref, qaug_ref, m_ref, l_ref, acc_ref, sbuf_ref,
                 *, seq):
    bs = MOBA_BLOCK
    tq = MOBA_TQ
    nb = seq // bs
    nl = V7X_LANES
    ncand = -(-nb // 8) * 8
    i = (tq // bs) * pl.program_id(2)
    exp2_scale = (HEAD_DIM ** -0.5) * 1.4426950408889634

    @pl.when(i == 0)
    def _():
        kf = k_ref[0]
        kaug_ref[:, :HEAD_DIM] = kf.astype(BF16)
        blk = lax.broadcasted_iota(jnp.int32, (seq, nl), 0) // bs
        col = lax.broadcasted_iota(jnp.int32, (seq, nl), 1)
        kaug_ref[:, HEAD_DIM:] = jnp.where(blk == col, MOBA_MASK_BIG, 0.0).astype(BF16)
        vb_ref[...] = v_ref[0].astype(BF16)
        rr = lax.broadcasted_iota(jnp.int32, (V7X_LANES, seq), 0)
        ss = lax.broadcasted_iota(jnp.int32, (V7X_LANES, seq), 1)
        onehot = jnp.where(ss // bs == rr, 1.0, 0.0).astype(BF16)
        tot = jnp.zeros((V7X_LANES, HEAD_DIM), F32)
        for piece in _split3(kf):
            tot = tot + _dot(onehot, piece)
        kmean = tot * (1.0 / bs)

        qaug_ref[:, :HEAD_DIM] = q_ref[0].astype(BF16)
        cand_idx = lax.broadcasted_iota(jnp.int32, (ncand, tq), 0)
        cand_f = cand_idx.astype(F32)
        for tile in range(seq // tq):
            rows = slice(tile * tq, (tile + 1) * tq)
            own = tile * (tq // bs) + lax.broadcasted_iota(jnp.int32, (ncand, tq), 1) // bs
            gate = jnp.where(cand_idx < own, _dot3_nt(kmean, q_ref[0, rows, :])[:ncand], NEG_INF)
            picked_t = jnp.zeros((ncand, tq), F32)
            for _ in range(MOBA_TOPK):
                mx = jnp.max(gate, axis=0, keepdims=True)
                first = jnp.min(jnp.where((gate == mx) & (mx > NEG_INF), cand_f, float(ncand)),
                                axis=0, keepdims=True)
                hit = cand_f == first
                picked_t = jnp.where(hit, 1.0, picked_t)
                gate = jnp.where(hit, NEG_INF, gate)
            allowed_t = jnp.where(cand_idx == own, 1.0, picked_t)
            allowed = jnp.concatenate([allowed_t, jnp.zeros((nl - ncand, tq), F32)], axis=0).T
            qaug_ref[rows, HEAD_DIM:] = (allowed - 1.0).astype(BF16)

    q_aug = qaug_ref[pl.ds(pl.multiple_of(i * bs, tq), tq), :]

    def raw_scores(first, nblocks):
        start = pl.multiple_of(first * bs, bs)
        return _dot_nt(q_aug, kaug_ref[pl.ds(start, nblocks * bs), :])

    def lane_groups(x):
        return [x[:, g * nl:(g + 1) * nl] for g in range(x.shape[1] // nl)]

    def sweep_past(step):
        wide = MOBA_SWEEP_BLOCKS[0]

        def wide_body(t, c):
            step(t * wide, wide)
            return c

        lax.fori_loop(0, i // wide, wide_body, 0)
        for w in MOBA_SWEEP_BLOCKS[1:]:
            @pl.when((i // w) % 2 == 1)
            def _(w=w):
                step((i // (2 * w)) * (2 * w), w)

    own0 = pl.multiple_of(i * bs, tq)
    row2 = lax.broadcasted_iota(jnp.int32, (tq, tq), 0)
    col2 = lax.broadcasted_iota(jnp.int32, (tq, tq), 1)
    s_own = jnp.where(col2 <= row2, _dot_nt(q_aug, kaug_ref[pl.ds(own0, tq), :]), -MOBA_MASK_BIG)

    m_ref[...] = functools.reduce(jnp.maximum, lane_groups(s_own))

    def max_step(first, nblocks):
        sc = raw_scores(first, nblocks)
        sbuf_ref[:, pl.ds(pl.multiple_of(first * bs, bs), nblocks * bs)] = sc
        m_ref[...] = functools.reduce(jnp.maximum, [m_ref[...]] + lane_groups(sc))

    sweep_past(max_step)
    m_ref[...] = jnp.broadcast_to(jnp.max(m_ref[...], axis=-1, keepdims=True), (tq, nl))

    def accumulate(sc, first_row, init):
        mrow = m_ref[...]
        ps = [jnp.exp2((g - mrow) * exp2_scale) for g in lane_groups(sc)]
        psum = functools.reduce(lambda a, b_: a + b_, ps)
        pv = _dot(jnp.concatenate(ps, axis=1).astype(BF16), vb_ref[pl.ds(first_row, sc.shape[1]), :])
        if init:
            l_ref[...] = psum
            acc_ref[...] = pv
        else:
            l_ref[...] += psum
            acc_ref[...] += pv

    accumulate(s_own, own0, init=True)

    def acc_step(first, nblocks):
        start = pl.multiple_of(first * bs, bs)
        accumulate(sbuf_ref[:, pl.ds(start, nblocks * bs)], start, init=False)

    sweep_past(acc_step)
    o_ref[0] = (acc_ref[...] / jnp.sum(l_ref[...], axis=-1, keepdims=True)).astype(o_ref.dtype)


def _moba(proj, q_col, k_col, v_col):
    b, s, _ = proj.shape
    h = N_HEADS_HALF
    tq = MOBA_TQ
    assert s % (MOBA_BLOCK * MOBA_SWEEP_BLOCKS[0]) == 0 and s // MOBA_BLOCK <= V7X_LANES
    return pl.pallas_call(
        functools.partial(_moba_kernel, seq=s),
        out_shape=jax.ShapeDtypeStruct((b, s, GROUP_W), BF16),
        grid=(b, h, s // tq),
        in_specs=[pl.BlockSpec((1, s, HEAD_DIM), lambda bi, hi, i: (bi, 0, q_col + hi)),
                  pl.BlockSpec((1, s, HEAD_DIM), lambda bi, hi, i: (bi, 0, k_col + hi)),
                  pl.BlockSpec((1, s, HEAD_DIM), lambda bi, hi, i: (bi, 0, v_col + hi))],
        out_specs=pl.BlockSpec((1, tq, HEAD_DIM), lambda bi, hi, i: (bi, i, hi)),
        scratch_shapes=[pltpu.VMEM((s, HEAD_DIM + V7X_LANES), BF16), pltpu.VMEM((s, HEAD_DIM), BF16),
                        pltpu.VMEM((s, HEAD_DIM + V7X_LANES), BF16),
                        pltpu.VMEM((tq, V7X_LANES), F32), pltpu.VMEM((tq, V7X_LANES), F32),
                        pltpu.VMEM((tq, HEAD_DIM), F32), pltpu.VMEM((tq, s), F32)],
        compiler_params=_cparams("parallel", "parallel", "arbitrary"),
        name="moba_attn",
    )(proj, proj, proj)


DIL_COPY_ROWS = 256
DIL_BLOCKS_PER_STEP = 8


def _dil_kernel(q_ref, k_ref, v_ref, o_ref, qd_ref, kd_ref, vd_ref, tq_ref, tk_ref, tv_ref, obuf_ref, lbuf_ref,
                *, seq):
    span = DIL_SPAN
    scale = HEAD_DIM ** -0.5
    kd_ref[0:span, :] = jnp.zeros((span, HEAD_DIM), BF16)
    vd_ref[0:span, :] = jnp.zeros((span, HEAD_DIM), BF16)
    qi = lax.broadcasted_iota(jnp.int32, (span, 2 * span), 0)
    ki = lax.broadcasted_iota(jnp.int32, (span, 2 * span), 1)
    in_window = (ki >= qi) & (ki <= qi + span)

    for g, rate in enumerate(DIL_RATES):
        n = seq // rate
        copies_per_res = n // DIL_COPY_ROWS
        nblk = n // span

        prev = DIL_RATES[g - 1] if g > 0 else 1
        staged = prev > 1 and rate % prev == 0
        keep_f32 = rate > 1 and g + 1 < len(DIL_RATES) and DIL_RATES[g + 1] % rate == 0
        step_rows = rate // prev if staged else rate

        def copy(c, cc, rate=rate, copies_per_res=copies_per_res, prev=prev, staged=staged,
                 keep_f32=keep_f32, step_rows=step_rows):
            rho = c // copies_per_res
            within = (c % copies_per_res) * (DIL_COPY_ROWS * step_rows)
            if staged:
                src = (rho % prev) * (seq // prev) + rho // prev + within
            else:
                src = rho + within
            dst = pl.multiple_of(c * DIL_COPY_ROWS, DIL_COPY_ROWS)
            idx = pl.ds(src, DIL_COPY_ROWS) if step_rows == 1 else pl.ds(src, DIL_COPY_ROWS, stride=step_rows)
            for x_ref, t_ref, d_ref, off in ((q_ref, tq_ref, qd_ref, 0), (k_ref, tk_ref, kd_ref, span),
                                             (v_ref, tv_ref, vd_ref, span)):
                x = t_ref[idx, :] if staged else x_ref[0, idx, :]
                if keep_f32:
                    t_ref[pl.ds(dst, DIL_COPY_ROWS), :] = x
                d_ref[pl.ds(off + dst, DIL_COPY_ROWS), :] = x.astype(BF16)
            return cc

        lax.fori_loop(0, seq // DIL_COPY_ROWS, copy, 0)

        def blocks(step, cc, g=g, rate=rate, nblk=nblk):
            blks = [step * DIL_BLOCKS_PER_STEP + sub for sub in range(DIL_BLOCKS_PER_STEP)]
            r0s = [pl.multiple_of(blk * span, span) for blk in blks]
            raw = [_dot_nt(qd_ref[pl.ds(r0, span), :], kd_ref[pl.ds(r0, 2 * span), :]) for r0 in r0s]
            scs = [jnp.where(in_window & ((blk % nblk > 0) | (ki >= span)), s * scale, NEG_INF)
                   for blk, s in zip(blks, raw)]
            ms = [jnp.max(sc, axis=-1, keepdims=True) for sc in scs]
            pr = [jnp.exp(sc - m) for sc, m in zip(scs, ms)]
            dens = [jnp.sum(p, axis=-1, keepdims=True) for p in pr]
            pvs = [_dot(p.astype(BF16), vd_ref[pl.ds(r0, 2 * span), :]) for p, r0 in zip(pr, r0s)]
            for blk, pv, m, den in zip(blks, pvs, ms, dens):
                dst = blk // nblk + (blk % nblk) * (span * rate)
                idx = pl.ds(dst, span) if rate == 1 else pl.ds(dst, span, stride=rate)
                obuf_ref[g, idx, :] = pv / den
                lbuf_ref[g, idx, :] = jnp.broadcast_to(m + jnp.log(den), (span, HEAD_DIM))
            return cc

        lax.fori_loop(0, seq // span // DIL_BLOCKS_PER_STEP, blocks, 0)

    def merge(t, c):
        r0 = pl.multiple_of(t * DIL_COPY_ROWS, DIL_COPY_ROWS)
        sl = pl.ds(r0, DIL_COPY_ROWS)
        ls = [lbuf_ref[g, sl, :] for g in range(len(DIL_RATES))]
        mx = functools.reduce(jnp.maximum, ls)
        es = [jnp.exp(l - mx) for l in ls]
        num = functools.reduce(lambda a, b_: a + b_, [e * obuf_ref[g, sl, :] for g, e in enumerate(es)])
        den = functools.reduce(lambda a, b_: a + b_, es)
        o_ref[0, sl, :] = (num / den).astype(o_ref.dtype)
        return c

    lax.fori_loop(0, seq // DIL_COPY_ROWS, merge, 0)


def _dilated(proj, q_col, k_col, v_col):
    b, s, _ = proj.shape
    h = N_HEADS_HALF
    ng = len(DIL_RATES)
    full = lambda col: pl.BlockSpec((1, s, HEAD_DIM), lambda bi, hi: (bi, 0, col + hi))
    return pl.pallas_call(
        functools.partial(_dil_kernel, seq=s),
        out_shape=jax.ShapeDtypeStruct((b, s, GROUP_W), BF16),
        grid=(b, h),
        in_specs=[full(q_col), full(k_col), full(v_col)],
        out_specs=pl.BlockSpec((1, s, HEAD_DIM), lambda bi, hi: (bi, 0, hi)),
        scratch_shapes=[pltpu.VMEM((s, HEAD_DIM), BF16),
                        pltpu.VMEM((s + DIL_SPAN, HEAD_DIM), BF16),
                        pltpu.VMEM((s + DIL_SPAN, HEAD_DIM), BF16),
                        pltpu.VMEM((s, HEAD_DIM), F32), pltpu.VMEM((s, HEAD_DIM), F32),
                        pltpu.VMEM((s, HEAD_DIM), F32),
                        pltpu.VMEM((ng, s, HEAD_DIM), F32), pltpu.VMEM((ng, s, HEAD_DIM), F32)],
        compiler_params=_cparams("parallel", "parallel"),
        name="dilated_attn",
    )(proj, proj, proj)


def _residual_epilogue(y, h_ref, gpost_ref, gnext_ref, h_out_ref, u_out_ref):
    hn = h_ref[...] + _rms(y, gpost_ref[...])
    h_out_ref[...] = hn
    if u_out_ref is not None:
        u_out_ref[...] = _rms(hn, gnext_ref[...]).astype(u_out_ref.dtype)


def _outproj_kernel(a_ref, b_ref, wa_ref, wb_ref, h_ref, gpost_ref, gnext_ref, h_out_ref, u_out_ref):
    y = _dot(a_ref[...], wa_ref[...]) + _dot(b_ref[...], wb_ref[...])
    _residual_epilogue(y, h_ref, gpost_ref, gnext_ref, h_out_ref, u_out_ref)


def _outproj(a, b, wa, wb, h, g_post, g_next, tm=OUTPROJ_TM):
    m, d = h.shape
    ka = a.shape[1]
    kb = b.shape[1]
    rowblk = lambda w: pl.BlockSpec((tm, w), lambda i: (i, 0))
    const = lambda r, c: pl.BlockSpec((r, c), lambda i: (0, 0))
    return pl.pallas_call(
        _outproj_kernel,
        out_shape=(jax.ShapeDtypeStruct((m, d), F32), jax.ShapeDtypeStruct((m, d), BF16)),
        grid=(m // tm,),
        in_specs=[rowblk(ka), rowblk(kb), const(ka, d), const(kb, d), rowblk(d), const(1, d), const(1, d)],
        out_specs=(rowblk(d), rowblk(d)),
        compiler_params=_cparams("parallel"),
        name="outproj_residual",
    )(a, b, wa, wb, h, g_post.reshape(1, d), g_next.reshape(1, d))


def _ffn_kernel(u_ref, wg_ref, wu_ref, wd_ref, h_ref, gpost_ref, gnext_ref, *rest, emit_next):
    if emit_next:
        h_out_ref, u_out_ref, acc_ref = rest
    else:
        h_out_ref, acc_ref = rest
        u_out_ref = None
    f = pl.program_id(1)

    @pl.when(f == 0)
    def _():
        acc_ref[...] = jnp.zeros_like(acc_ref)

    u = u_ref[...]
    gate = _dot(u, wg_ref[...])
    up = _dot(u, wu_ref[...])
    act = (gate * _sigmoid(gate) * up).astype(BF16)
    acc_ref[...] += _dot(act, wd_ref[...])

    @pl.when(f == pl.num_programs(1) - 1)
    def _():
        _residual_epilogue(acc_ref[...], h_ref, gpost_ref, gnext_ref, h_out_ref, u_out_ref)


def _ffn(u, wg, wu, wd, h, g_post, g_next, emit_next, tm=FFN_TM, tf=FFN_TF):
    m, d = h.shape
    ff = wg.shape[1]
    rowblk = pl.BlockSpec((tm, d), lambda i, f: (i, 0))
    const = pl.BlockSpec((1, d), lambda i, f: (0, 0))
    out_shape = [jax.ShapeDtypeStruct((m, d), F32)]
    out_specs = [rowblk]
    if emit_next:
        out_shape.append(jax.ShapeDtypeStruct((m, d), BF16))
        out_specs.append(rowblk)
    res = pl.pallas_call(
        functools.partial(_ffn_kernel, emit_next=emit_next),
        out_shape=tuple(out_shape),
        grid=(m // tm, ff // tf),
        in_specs=[rowblk,
                  pl.BlockSpec((d, tf), lambda i, f: (0, f)),
                  pl.BlockSpec((d, tf), lambda i, f: (0, f)),
                  pl.BlockSpec((tf, d), lambda i, f: (f, 0)),
                  rowblk, const, const],
        out_specs=tuple(out_specs),
        scratch_shapes=[pltpu.VMEM((tm, d), F32)],
        compiler_params=_cparams("parallel", "arbitrary"),
        name="swiglu_residual",
    )(u, wg, wu, wd, h, g_post.reshape(1, d), g_next.reshape(1, d))
    return res if emit_next else (res[0], None)


def _pad_lanes(vec, offset=0):
    out = jnp.zeros((1, V7X_LANES), F32)
    return lax.dynamic_update_slice(out, vec.reshape(1, -1).astype(F32), (0, offset))


def kernel(x, mix_norm_pre, mix_norm_post, ffn_norm_pre, ffn_norm_post, ev_w_in, ev_conv_w, ev_a_log,
           ev_dt_bias, ev_onorm, ev_w_out, od_w_in, od_w_out, ffn_w_gate, ffn_w_up, ffn_w_down):
    b, s, d = x.shape
    m = b * s
    gw = GROUP_W
    nh = N_HEADS_HALF
    h = x.reshape(m, d)

    u = _prenorm(h, mix_norm_pre[0])
    w_in = ev_w_in[0]
    n_sb = 3 * gw
    n_gdn = 4 * gw
    proj_sb = _matmul(u, w_in[:, :n_sb].astype(BF16), out_dtype=BF16).reshape(b, s, n_sb)
    proj_gdn = _matmul(u, w_in[:, n_sb:n_sb + n_gdn].astype(BF16)).reshape(b, s, n_gdn)
    w_ab = jnp.pad(w_in[:, n_sb + n_gdn:], ((0, 0), (0, V7X_LANES - 2 * nh))).astype(BF16)
    proj_ab = _matmul(u, w_ab).reshape(b, s, V7X_LANES)
    o_sb = _sb_attention(proj_sb, 0, nh, 2 * nh)
    o_gdn = _gdn(proj_gdn, proj_ab, ev_conv_w[0], _pad_lanes(ev_a_log[0]), _pad_lanes(ev_dt_bias[0]),
                 ev_onorm[0].reshape(1, HEAD_DIM))
    w_out = ev_w_out[0].astype(BF16)
    h, u = _outproj(o_sb.reshape(m, gw), o_gdn.reshape(m, gw), w_out[:gw], w_out[gw:], h,
                    mix_norm_post[0], ffn_norm_pre[0])
    h, u = _ffn(u, ffn_w_gate[0].astype(BF16), ffn_w_up[0].astype(BF16), ffn_w_down[0].astype(BF16), h,
                ffn_norm_post[0], mix_norm_pre[1], emit_next=True)

    proj_od = _matmul(u, od_w_in[0].astype(BF16)).reshape(b, s, 6 * gw)
    o_moba = _moba(proj_od, 0, nh, 2 * nh)
    o_dil = _dilated(proj_od, 3 * nh, 4 * nh, 5 * nh)
    w_out = od_w_out[0].astype(BF16)
    h, u = _outproj(o_moba.reshape(m, gw), o_dil.reshape(m, gw), w_out[:gw], w_out[gw:], h,
                    mix_norm_post[1], ffn_norm_pre[1])
    h, _ = _ffn(u, ffn_w_gate[1].astype(BF16), ffn_w_up[1].astype(BF16), ffn_w_down[1].astype(BF16), h,
                ffn_norm_post[1], ffn_norm_post[1], emit_next=False)
    return h.reshape(b, s, d)
```

```python
import functools

import jax
import jax.numpy as jnp
from jax import lax
from jax.experimental import pallas as pl
from jax.experimental.pallas import tpu as pltpu

F32 = jnp.float32
BF16 = jnp.bfloat16

HEAD_DIM = 128
N_HEADS_HALF = 8
GROUP_W = N_HEADS_HALF * HEAD_DIM
NORM_EPS = 1e-6
GDN_CHUNK = 64
GDN_CONV = 4
MOBA_BLOCK = 256
MOBA_TOPK = 3
DIL_SPAN = 128
DIL_RATES = (1, 4, 16)

V7X_LANES = 128
V7X_VMEM_LIMIT_BYTES = 56 * 1024 * 1024

PRENORM_TM = 512
PROJ_TM, PROJ_TN = 2048, 512
OUTPROJ_TM = 512
FFN_TM, FFN_TF = 512, 512

NEG_INF = float("-inf")


def _cparams(*sem):
    return pltpu.CompilerParams(dimension_semantics=sem, vmem_limit_bytes=V7X_VMEM_LIMIT_BYTES)


def _dot(a, b):
    return jnp.dot(a, b, preferred_element_type=F32)


def _dot_nt(a, b):
    return lax.dot_general(a, b, (((1,), (1,)), ((), ())), preferred_element_type=F32)


def _dot_tn(a, b):
    return lax.dot_general(a, b, (((0,), (0,)), ((), ())), preferred_element_type=F32)


def _split2(x):
    hi = x.astype(BF16)
    lo = (x - hi.astype(F32)).astype(BF16)
    return hi, lo


def _split3(x):
    p1 = x.astype(BF16)
    r1 = x - p1.astype(F32)
    p2 = r1.astype(BF16)
    p3 = (r1 - p2.astype(F32)).astype(BF16)
    return p1, p2, p3


def _dot3_nt(a, b):
    a1, a2 = _split2(a)
    b1, b2 = _split2(b)
    return _dot_nt(a1, b1) + _dot_nt(a1, b2) + _dot_nt(a2, b1)


def _softplus(x):
    return jnp.maximum(x, 0.0) + jnp.log1p(jnp.exp(-jnp.abs(x)))


def _sigmoid(x):
    return 1.0 / (1.0 + jnp.exp(-x))


def _rms(x, gain):
    return x * lax.rsqrt(jnp.mean(x * x, axis=-1, keepdims=True) + NORM_EPS) * gain


def _prenorm_kernel(x_ref, g_ref, o_ref):
    o_ref[...] = _rms(x_ref[...], g_ref[...]).astype(o_ref.dtype)


def _prenorm(x2d, gain, tm=PRENORM_TM):
    m, d = x2d.shape
    return pl.pallas_call(
        _prenorm_kernel,
        out_shape=jax.ShapeDtypeStruct((m, d), BF16),
        grid=(m // tm,),
        in_specs=[pl.BlockSpec((tm, d), lambda i: (i, 0)),
                  pl.BlockSpec((1, d), lambda i: (0, 0))],
        out_specs=pl.BlockSpec((tm, d), lambda i: (i, 0)),
        compiler_params=_cparams("parallel"),
        name="prenorm",
    )(x2d, gain.reshape(1, d))


def _matmul_kernel(x_ref, w_ref, o_ref):
    o_ref[...] = _dot(x_ref[...], w_ref[...]).astype(o_ref.dtype)


def _matmul(x, w, out_dtype=F32, tm=PROJ_TM, tn=PROJ_TN):
    m, k = x.shape
    n = w.shape[1]
    tn = min(tn, n)
    return pl.pallas_call(
        _matmul_kernel,
        out_shape=jax.ShapeDtypeStruct((m, n), out_dtype),
        grid=(m // tm, n // tn),
        in_specs=[pl.BlockSpec((tm, k), lambda i, j: (i, 0)),
                  pl.BlockSpec((k, tn), lambda i, j: (0, j))],
        out_specs=pl.BlockSpec((tm, tn), lambda i, j: (i, j)),
        compiler_params=_cparams("parallel", "arbitrary"),
        name="proj_matmul",
    )(x, w)


SB_TQ = 512
SB_TK = 512
SB_PAIR = 2 * V7X_LANES


def _sb_sweep_step(q, kb_ref, vb_ref, acc_ref, carry_ref, suffix_op, start, qpos, masked):
    scale = HEAD_DIM ** -0.5
    log2e = 1.4426950408889634
    nl = V7X_LANES
    pairs = range(SB_TK // SB_PAIR - 1, -1, -1)
    p0s = [pl.multiple_of(start + pair * SB_PAIR, SB_PAIR) for pair in pairs]
    raws = [_dot_nt(q, kb_ref[pl.ds(p0, SB_PAIR), :]) for p0 in p0s]
    log_sig, pasts, parts = [], [], []
    for p0, raw in zip(p0s, raws):
        z = raw * scale
        sp = jnp.maximum(z, 0.0) + jnp.log(1.0 + jnp.exp2(jnp.abs(raw) * (-scale * log2e)))
        log_sig.append(z - sp)
        if masked:
            past = (p0 + lax.broadcasted_iota(jnp.int32, (SB_TQ, SB_PAIR), 1)) < qpos
            sp = jnp.where(past, sp, 0.0)
            pasts.append(past)
        hi, lo = _split2(sp)
        parts.append((jnp.concatenate([hi[:, nl:], lo[:, nl:]], axis=1),
                      jnp.concatenate([hi[:, :nl], lo[:, :nl]], axis=1)))
    sums = [(_dot(near, suffix_op), _dot(far, suffix_op)) for near, far in parts]
    carry = carry_ref[...]
    betweens = []
    for sr_near, sr_far in sums:
        c_far = carry + sr_near[:, nl:]
        betweens.append(jnp.concatenate([sr_far[:, :nl] + c_far, sr_near[:, :nl] + carry], axis=1))
        carry = c_far + sr_far[:, nl:]
    carry_ref[...] = carry
    ws = []
    for idx, (ls, between) in enumerate(zip(log_sig, betweens)):
        w = jnp.exp2((ls - between) * log2e)
        if masked:
            w = jnp.where(pasts[idx], w, 0.0)
        ws.append(w.astype(BF16))
    acc_ref[...] += sum(_dot(w, vb_ref[pl.ds(p0, SB_PAIR), :]) for w, p0 in zip(ws, p0s))


def _sb_kernel(q_ref, k_ref, v_ref, o_ref, kb_ref, vb_ref, acc_ref, carry_ref):
    i = pl.program_id(2)

    @pl.when(i == 0)
    def _():
        kb_ref[...] = k_ref[0].astype(BF16)
        vb_ref[...] = v_ref[0].astype(BF16)

    q = q_ref[0].astype(BF16)
    acc_ref[...] = jnp.zeros_like(acc_ref)
    carry_ref[...] = jnp.zeros_like(carry_ref)

    kk = lax.broadcasted_iota(jnp.int32, (SB_PAIR, SB_PAIR), 0) % V7X_LANES
    cc = lax.broadcasted_iota(jnp.int32, (SB_PAIR, SB_PAIR), 1)
    suffix_op = jnp.where((cc >= V7X_LANES) | (kk > cc), 1.0, 0.0).astype(BF16)

    qpos = i * SB_TQ + lax.broadcasted_iota(jnp.int32, (SB_TQ, SB_PAIR), 0)
    diag = (i * SB_TQ) // SB_TK
    _sb_sweep_step(q, kb_ref, vb_ref, acc_ref, carry_ref, suffix_op, diag * SB_TK, qpos, masked=True)

    def body(t, c):
        _sb_sweep_step(q, kb_ref, vb_ref, acc_ref, carry_ref, suffix_op, (diag - t) * SB_TK, None, masked=False)
        return c

    lax.fori_loop(1, diag + 1, body, 0)
    o_ref[0] = acc_ref[...].astype(o_ref.dtype)


def _sb_attention(proj, q_col, k_col, v_col):
    b, s, _ = proj.shape
    h = N_HEADS_HALF
    return pl.pallas_call(
        _sb_kernel,
        out_shape=jax.ShapeDtypeStruct((b, s, GROUP_W), BF16),
        grid=(b, h, s // SB_TQ),
        in_specs=[pl.BlockSpec((1, SB_TQ, HEAD_DIM), lambda bi, hi, i: (bi, i, q_col + hi)),
                  pl.BlockSpec((1, s, HEAD_DIM), lambda bi, hi, i: (bi, 0, k_col + hi)),
                  pl.BlockSpec((1, s, HEAD_DIM), lambda bi, hi, i: (bi, 0, v_col + hi))],
        out_specs=pl.BlockSpec((1, SB_TQ, HEAD_DIM), lambda bi, hi, i: (bi, i, hi)),
        scratch_shapes=[pltpu.VMEM((s, HEAD_DIM), BF16), pltpu.VMEM((s, HEAD_DIM), BF16),
                        pltpu.VMEM((SB_TQ, HEAD_DIM), F32), pltpu.VMEM((SB_TQ, HEAD_DIM), F32)],
        compiler_params=_cparams("parallel", "parallel", "arbitrary"),
        name="stickbreak_attn",
    )(proj, proj, proj)


GDN_ROWS = 512
GDN_GROUP = 256
GDN_HALO = 8
GDN_HEADS_PER_STEP = 8


def _gdn_prep(hd, xq_ref, xk_ref, xv_ref, ab_ref, wq_ref, wk_ref, wv_ref, alog_ref, dtb_ref, xbuf_ref):
    h = pl.program_id(1) * GDN_HEADS_PER_STEP + hd
    hl = slice(hd * HEAD_DIM, (hd + 1) * HEAD_DIM)
    rows = GDN_ROWS
    c = GDN_CHUNK

    conv = []
    for idx, (x_ref, cw_ref) in enumerate(((xq_ref, wq_ref), (xk_ref, wk_ref), (xv_ref, wv_ref))):
        xbuf_ref[hd, idx, GDN_HALO:GDN_HALO + rows, :] = x_ref[0, :, hl]
        y = jnp.zeros((rows, HEAD_DIM), F32)
        for tap in range(GDN_CONV):
            shift = GDN_CONV - 1 - tap
            y = y + cw_ref[tap:tap + 1, hl] * xbuf_ref[hd, idx, GDN_HALO - shift:GDN_HALO - shift + rows, :]
        xbuf_ref[hd, idx, 0:GDN_HALO, :] = xbuf_ref[hd, idx, rows:rows + GDN_HALO, :]
        conv.append(y * _sigmoid(y))
    yq, yk, yv = conv
    q = yq * lax.rsqrt(jnp.sum(yq * yq, axis=-1, keepdims=True) + NORM_EPS) * (HEAD_DIM ** -0.5)
    k = yk * lax.rsqrt(jnp.sum(yk * yk, axis=-1, keepdims=True) + NORM_EPS)
    v = yv

    ab = ab_ref[0]
    g_all = -jnp.exp(alog_ref[...]) * _softplus(ab + dtb_ref[...])
    beta_all = _sigmoid(ab)
    ab_lane = lax.broadcasted_iota(jnp.int32, (rows, V7X_LANES), 1)
    g_col = jnp.sum(jnp.where(ab_lane == h, g_all, 0.0), axis=-1, keepdims=True)
    beta = jnp.sum(jnp.where(ab_lane == h + N_HEADS_HALF, beta_all, 0.0), axis=-1, keepdims=True)
    g_rep = jnp.broadcast_to(g_col, (rows, HEAD_DIM))
    ri = lax.broadcasted_iota(jnp.int32, (GDN_GROUP, GDN_GROUP), 0)
    ci = lax.broadcasted_iota(jnp.int32, (GDN_GROUP, GDN_GROUP), 1)
    cum_op = jnp.where((ri // c == ci // c) & (ci <= ri), 1.0, 0.0).astype(BF16)
    gc = jnp.concatenate(
        [sum(_dot(cum_op, piece) for piece in _split3(g_rep[r0:r0 + GDN_GROUP]))
         for r0 in range(0, rows, GDN_GROUP)], axis=0)

    gc3 = gc.reshape(rows // c, c, HEAD_DIM)
    gl = jnp.broadcast_to(gc3[:, c - 1:c, :], (rows // c, c, HEAD_DIM)).reshape(rows, HEAD_DIM)
    eg = jnp.exp(gc)
    egl = jnp.exp(gl)
    kd = k * jnp.exp(gl - gc)
    qg = q * eg
    kb = k * beta
    rhs_uw = jnp.concatenate([v * beta, kb * eg], axis=1)
    return dict(q=q, k=k, kb=kb, kd=kd, qg=qg, gc=gc, egl=egl, rhs_uw=rhs_uw)


def _gdn_kernel(xq_ref, xk_ref, xv_ref, z_ref, ab_ref, wq_ref, wk_ref, wv_ref, alog_ref, dtb_ref,
                onorm_ref, o_ref, xbuf_ref, state_ref, u_ref, w_ref, vnew_ref, ointer_ref):
    rows = GDN_ROWS
    c = GDN_CHUNK
    nch = rows // c
    heads = range(GDN_HEADS_PER_STEP)
    groups = [(hd, slice(r0, r0 + GDN_GROUP)) for hd in heads for r0 in range(0, rows, GDN_GROUP)]

    @pl.when(pl.program_id(2) == 0)
    def _():
        state_ref[...] = jnp.zeros_like(state_ref)
        xbuf_ref[:, :, 0:GDN_HALO, :] = jnp.zeros((GDN_HEADS_PER_STEP, 3, GDN_HALO, HEAD_DIM), F32)

    hv = [_gdn_prep(hd, xq_ref, xk_ref, xv_ref, ab_ref, wq_ref, wk_ref, wv_ref, alog_ref, dtb_ref, xbuf_ref)
          for hd in heads]

    lane = lax.broadcasted_iota(jnp.int32, (GDN_GROUP, V7X_LANES), 1)
    gi = lax.broadcasted_iota(jnp.int32, (GDN_GROUP, GDN_GROUP), 0)
    gj = lax.broadcasted_iota(jnp.int32, (GDN_GROUP, GDN_GROUP), 1)
    same = (gi // c) == (gj // c)
    lower_incl = same & (gj <= gi)
    lower_strict = same & (gj < gi)
    eye = jnp.where(gi == gj, 1.0, 0.0)

    ps, xs, intras = [], [], []
    for hd, sl in groups:
        p1, p2, p3 = (piece.astype(F32) for piece in _split3(hv[hd]["gc"][sl]))
        lhs = jnp.where(lane == 0, p1, jnp.where(lane == 1, p2, jnp.where(lane == 2, p3,
              jnp.where(lane < 6, 1.0, 0.0))))
        rhs = jnp.where(lane < 3, 1.0, jnp.where(lane == 3, -p1, jnp.where(lane == 4, -p2,
              jnp.where(lane == 5, -p3, 0.0))))
        diff = _dot_nt(lhs.astype(BF16), rhs.astype(BF16))
        decay = jnp.exp(jnp.where(lower_incl, diff, 0.0))
        kgb = hv[hd]["k"][sl].astype(BF16)
        a_mat = jnp.where(lower_strict, _dot_nt(hv[hd]["kb"][sl].astype(BF16), kgb) * decay, 0.0)
        intras.append(jnp.where(lower_incl, _dot_nt(hv[hd]["q"][sl].astype(BF16), kgb) * decay, 0.0).astype(BF16))
        ps.append(-a_mat)
        xs.append(eye - a_mat)

    for _ in range(5):
        pbs = [p.astype(BF16) for p in ps]
        ps = [_dot(pb, pb) for pb in pbs]
        xs = [x + _dot(x.astype(BF16), p.astype(BF16)) for x, p in zip(xs, ps)]
    for (hd, sl), x in zip(groups, xs):
        uw = _dot(x.astype(BF16), hv[hd]["rhs_uw"][sl].astype(BF16))
        u_ref[hd, sl, :] = uw[:, :HEAD_DIM]
        w_ref[hd, sl, :] = uw[:, HEAD_DIM:]

    chunk = lambda ch: slice(ch * c, (ch + 1) * c)
    trans = [[None] * nch for _ in heads]
    drive = [[None] * nch for _ in heads]
    for ch in range(nch):
        for hd in heads:
            kdb = hv[hd]["kd"][chunk(ch)].astype(BF16)
            trans[hd][ch] = _dot_tn(kdb, w_ref[hd, chunk(ch), :].astype(BF16)).astype(BF16)
            drive[hd][ch] = _dot_tn(kdb, u_ref[hd, chunk(ch), :].astype(BF16))
    st = [state_ref[hd] for hd in heads]
    states = [[None] * nch for _ in heads]
    for ch in range(nch):
        for hd in heads:
            stb = st[hd].astype(BF16)
            states[hd][ch] = stb
            st[hd] = st[hd] * hv[hd]["egl"][ch * c:ch * c + 1, :] - _dot(trans[hd][ch], stb) + drive[hd][ch]
    for hd in heads:
        state_ref[hd] = st[hd]
    for ch in range(nch):
        for hd in heads:
            lhs = jnp.concatenate([w_ref[hd, chunk(ch), :], hv[hd]["qg"][chunk(ch)]], axis=0).astype(BF16)
            ws = _dot(lhs, states[hd][ch])
            vnew_ref[hd, chunk(ch), :] = u_ref[hd, chunk(ch), :] - ws[:c]
            ointer_ref[hd, chunk(ch), :] = ws[c:]

    for (hd, sl), intra in zip(groups, intras):
        hl = slice(hd * HEAD_DIM, (hd + 1) * HEAD_DIM)
        zz = z_ref[0, sl, hl]
        o = ointer_ref[hd, sl, :] + _dot(intra, vnew_ref[hd, sl, :].astype(BF16))
        o_ref[0, sl, hl] = (_rms(o, onorm_ref[...]) * (zz * _sigmoid(zz))).astype(o_ref.dtype)


def _gdn(proj, ab, conv_w, alog_row, dtb_row, onorm_row):
    b, s, _ = proj.shape
    nhd = GDN_HEADS_PER_STEP
    wblk = nhd * HEAD_DIM
    hsteps = N_HEADS_HALF // nhd
    row = lambda col: pl.BlockSpec((1, GDN_ROWS, wblk), lambda bi, hi, si: (bi, si, col + hi))
    cw = lambda col: pl.BlockSpec((GDN_CONV, wblk), lambda bi, hi, si: (0, col + hi))
    vec = pl.BlockSpec((1, V7X_LANES), lambda bi, hi, si: (0, 0))
    per_head = lambda *shape: pltpu.VMEM((nhd,) + shape, F32)
    return pl.pallas_call(
        _gdn_kernel,
        out_shape=jax.ShapeDtypeStruct((b, s, GROUP_W), BF16),
        grid=(b, hsteps, s // GDN_ROWS),
        in_specs=[row(0), row(hsteps), row(2 * hsteps), row(3 * hsteps),
                  pl.BlockSpec((1, GDN_ROWS, V7X_LANES), lambda bi, hi, si: (bi, si, 0)),
                  cw(0), cw(hsteps), cw(2 * hsteps), vec, vec, vec],
        out_specs=pl.BlockSpec((1, GDN_ROWS, wblk), lambda bi, hi, si: (bi, si, hi)),
        scratch_shapes=[per_head(3, GDN_ROWS + GDN_HALO, HEAD_DIM),
                        per_head(HEAD_DIM, HEAD_DIM),
                        per_head(GDN_ROWS, HEAD_DIM), per_head(GDN_ROWS, HEAD_DIM),
                        per_head(GDN_ROWS, HEAD_DIM), per_head(GDN_ROWS, HEAD_DIM)],
        compiler_params=_cparams("parallel", "parallel", "arbitrary"),
        name="gated_delta",
    )(proj, proj, proj, proj, ab, conv_w, conv_w, conv_w, alog_row, dtb_row, onorm_row)


MOBA_MASK_BIG = 1e30
MOBA_TQ = 2 * MOBA_BLOCK
MOBA_SWEEP_BLOCKS = (8, 4, 2)


def _moba_kernel(q_ref, k_ref, v_ref, o_ref, kaug_ref, vb_ref, qaug_ref, m_ref, l_ref, acc_ref, sbuf_ref,
                 *, seq):
    bs = MOBA_BLOCK
    tq = MOBA_TQ
    nb = seq // bs
    nl = V7X_LANES
    ncand = -(-nb // 8) * 8
    i = (tq // bs) * pl.program_id(2)
    exp2_scale = (HEAD_DIM ** -0.5) * 1.4426950408889634

    @pl.when(i == 0)
    def _():
        kf = k_ref[0]
        kaug_ref[:, :HEAD_DIM] = kf.astype(BF16)
        blk = lax.broadcasted_iota(jnp.int32, (seq, nl), 0) // bs
        col = lax.broadcasted_iota(jnp.int32, (seq, nl), 1)
        kaug_ref[:, HEAD_DIM:] = jnp.where(blk == col, MOBA_MASK_BIG, 0.0).astype(BF16)
        vb_ref[...] = v_ref[0].astype(BF16)
        rr = lax.broadcasted_iota(jnp.int32, (V7X_LANES, seq), 0)
        ss = lax.broadcasted_iota(jnp.int32, (V7X_LANES, seq), 1)
        onehot = jnp.where(ss // bs == rr, 1.0, 0.0).astype(BF16)
        tot = jnp.zeros((V7X_LANES, HEAD_DIM), F32)
        for piece in _split3(kf):
            tot = tot + _dot(onehot, piece)
        kmean = tot * (1.0 / bs)

        qaug_ref[:, :HEAD_DIM] = q_ref[0].astype(BF16)
        cand_idx = lax.broadcasted_iota(jnp.int32, (ncand, tq), 0)
        cand_f = cand_idx.astype(F32)
        for tile in range(seq // tq):
            rows = slice(tile * tq, (tile + 1) * tq)
            own = tile * (tq // bs) + lax.broadcasted_iota(jnp.int32, (ncand, tq), 1) // bs
            gate = jnp.where(cand_idx < own, _dot3_nt(kmean, q_ref[0, rows, :])[:ncand], NEG_INF)
            picked_t = jnp.zeros((ncand, tq), F32)
            for _ in range(MOBA_TOPK):
                mx = jnp.max(gate, axis=0, keepdims=True)
                first = jnp.min(jnp.where((gate == mx) & (mx > NEG_INF), cand_f, float(ncand)),
                                axis=0, keepdims=True)
                hit = cand_f == first
                picked_t = jnp.where(hit, 1.0, picked_t)
                gate = jnp.where(hit, NEG_INF, gate)
            allowed_t = jnp.where(cand_idx == own, 1.0, picked_t)
            allowed = jnp.concatenate([allowed_t, jnp.zeros((nl - ncand, tq), F32)], axis=0).T
            qaug_ref[rows, HEAD_DIM:] = (allowed - 1.0).astype(BF16)

    q_aug = qaug_ref[pl.ds(pl.multiple_of(i * bs, tq), tq), :]

    def raw_scores(first, nblocks):
        start = pl.multiple_of(first * bs, bs)
        return _dot_nt(q_aug, kaug_ref[pl.ds(start, nblocks * bs), :])

    def lane_groups(x):
        return [x[:, g * nl:(g + 1) * nl] for g in range(x.shape[1] // nl)]

    def sweep_past(step):
        wide = MOBA_SWEEP_BLOCKS[0]

        def wide_body(t, c):
            step(t * wide, wide)
            return c

        lax.fori_loop(0, i // wide, wide_body, 0)
        for w in MOBA_SWEEP_BLOCKS[1:]:
            @pl.when((i // w) % 2 == 1)
            def _(w=w):
                step((i // (2 * w)) * (2 * w), w)

    own0 = pl.multiple_of(i * bs, tq)
    row2 = lax.broadcasted_iota(jnp.int32, (tq, tq), 0)
    col2 = lax.broadcasted_iota(jnp.int32, (tq, tq), 1)
    s_own = jnp.where(col2 <= row2, _dot_nt(q_aug, kaug_ref[pl.ds(own0, tq), :]), -MOBA_MASK_BIG)

    m_ref[...] = functools.reduce(jnp.maximum, lane_groups(s_own))

    def max_step(first, nblocks):
        sc = raw_scores(first, nblocks)
        sbuf_ref[:, pl.ds(pl.multiple_of(first * bs, bs), nblocks * bs)] = sc
        m_ref[...] = functools.reduce(jnp.maximum, [m_ref[...]] + lane_groups(sc))

    sweep_past(max_step)
    m_ref[...] = jnp.broadcast_to(jnp.max(m_ref[...], axis=-1, keepdims=True), (tq, nl))

    def accumulate(sc, first_row, init):
        mrow = m_ref[...]
        ps = [jnp.exp2((g - mrow) * exp2_scale) for g in lane_groups(sc)]
        psum = functools.reduce(lambda a, b_: a + b_, ps)
        pv = _dot(jnp.concatenate(ps, axis=1).astype(BF16), vb_ref[pl.ds(first_row, sc.shape[1]), :])
        if init:
            l_ref[...] = psum
            acc_ref[...] = pv
        else:
            l_ref[...] += psum
            acc_ref[...] += pv

    accumulate(s_own, own0, init=True)

    def acc_step(first, nblocks):
        start = pl.multiple_of(first * bs, bs)
        accumulate(sbuf_ref[:, pl.ds(start, nblocks * bs)], start, init=False)

    sweep_past(acc_step)
    o_ref[0] = (acc_ref[...] / jnp.sum(l_ref[...], axis=-1, keepdims=True)).astype(o_ref.dtype)


def _moba(proj, q_col, k_col, v_col):
    b, s, _ = proj.shape
    h = N_HEADS_HALF
    tq = MOBA_TQ
    assert s % (MOBA_BLOCK * MOBA_SWEEP_BLOCKS[0]) == 0 and s // MOBA_BLOCK <= V7X_LANES
    return pl.pallas_call(
        functools.partial(_moba_kernel, seq=s),
        out_shape=jax.ShapeDtypeStruct((b, s, GROUP_W), BF16),
        grid=(b, h, s // tq),
        in_specs=[pl.BlockSpec((1, s, HEAD_DIM), lambda bi, hi, i: (bi, 0, q_col + hi)),
                  pl.BlockSpec((1, s, HEAD_DIM), lambda bi, hi, i: (bi, 0, k_col + hi)),
                  pl.BlockSpec((1, s, HEAD_DIM), lambda bi, hi, i: (bi, 0, v_col + hi))],
        out_specs=pl.BlockSpec((1, tq, HEAD_DIM), lambda bi, hi, i: (bi, i, hi)),
        scratch_shapes=[pltpu.VMEM((s, HEAD_DIM + V7X_LANES), BF16), pltpu.VMEM((s, HEAD_DIM), BF16),
                        pltpu.VMEM((s, HEAD_DIM + V7X_LANES), BF16),
                        pltpu.VMEM((tq, V7X_LANES), F32), pltpu.VMEM((tq, V7X_LANES), F32),
                        pltpu.VMEM((tq, HEAD_DIM), F32), pltpu.VMEM((tq, s), F32)],
        compiler_params=_cparams("parallel", "parallel", "arbitrary"),
        name="moba_attn",
    )(proj, proj, proj)


DIL_COPY_ROWS = 256
DIL_BLOCKS_PER_STEP = 8


def _dil_kernel(q_ref, k_ref, v_ref, o_ref, qd_ref, kd_ref, vd_ref, tq_ref, tk_ref, tv_ref, obuf_ref, lbuf_ref,
                *, seq):
    span = DIL_SPAN
    scale = HEAD_DIM ** -0.5
    kd_ref[0:span, :] = jnp.zeros((span, HEAD_DIM), BF16)
    vd_ref[0:span, :] = jnp.zeros((span, HEAD_DIM), BF16)
    qi = lax.broadcasted_iota(jnp.int32, (span, 2 * span), 0)
    ki = lax.broadcasted_iota(jnp.int32, (span, 2 * span), 1)
    in_window = (ki >= qi) & (ki <= qi + span)

    for g, rate in enumerate(DIL_RATES):
        n = seq // rate
        copies_per_res = n // DIL_COPY_ROWS
        nblk = n // span

        prev = DIL_RATES[g - 1] if g > 0 else 1
        staged = prev > 1 and rate % prev == 0
        keep_f32 = rate > 1 and g + 1 < len(DIL_RATES) and DIL_RATES[g + 1] % rate == 0
        step_rows = rate // prev if staged else rate

        def copy(c, cc, rate=rate, copies_per_res=copies_per_res, prev=prev, staged=staged,
                 keep_f32=keep_f32, step_rows=step_rows):
            rho = c // copies_per_res
            within = (c % copies_per_res) * (DIL_COPY_ROWS * step_rows)
            if staged:
                src = (rho % prev) * (seq // prev) + rho // prev + within
            else:
                src = rho + within
            dst = pl.multiple_of(c * DIL_COPY_ROWS, DIL_COPY_ROWS)
            idx = pl.ds(src, DIL_COPY_ROWS) if step_rows == 1 else pl.ds(src, DIL_COPY_ROWS, stride=step_rows)
            for x_ref, t_ref, d_ref, off in ((q_ref, tq_ref, qd_ref, 0), (k_ref, tk_ref, kd_ref, span),
                                             (v_ref, tv_ref, vd_ref, span)):
                x = t_ref[idx, :] if staged else x_ref[0, idx, :]
                if keep_f32:
                    t_ref[pl.ds(dst, DIL_COPY_ROWS), :] = x
                d_ref[pl.ds(off + dst, DIL_COPY_ROWS), :] = x.astype(BF16)
            return cc

        lax.fori_loop(0, seq // DIL_COPY_ROWS, copy, 0)

        def blocks(step, cc, g=g, rate=rate, nblk=nblk):
            blks = [step * DIL_BLOCKS_PER_STEP + sub for sub in range(DIL_BLOCKS_PER_STEP)]
            r0s = [pl.multiple_of(blk * span, span) for blk in blks]
            raw = [_dot_nt(qd_ref[pl.ds(r0, span), :], kd_ref[pl.ds(r0, 2 * span), :]) for r0 in r0s]
            scs = [jnp.where(in_window & ((blk % nblk > 0) | (ki >= span)), s * scale, NEG_INF)
                   for blk, s in zip(blks, raw)]
            ms = [jnp.max(sc, axis=-1, keepdims=True) for sc in scs]
            pr = [jnp.exp(sc - m) for sc, m in zip(scs, ms)]
            dens = [jnp.sum(p, axis=-1, keepdims=True) for p in pr]
            pvs = [_dot(p.astype(BF16), vd_ref[pl.ds(r0, 2 * span), :]) for p, r0 in zip(pr, r0s)]
            for blk, pv, m, den in zip(blks, pvs, ms, dens):
                dst = blk // nblk + (blk % nblk) * (span * rate)
                idx = pl.ds(dst, span) if rate == 1 else pl.ds(dst, span, stride=rate)
                obuf_ref[g, idx, :] = pv / den
                lbuf_ref[g, idx, :] = jnp.broadcast_to(m + jnp.log(den), (span, HEAD_DIM))
            return cc

        lax.fori_loop(0, seq // span // DIL_BLOCKS_PER_STEP, blocks, 0)

    def merge(t, c):
        r0 = pl.multiple_of(t * DIL_COPY_ROWS, DIL_COPY_ROWS)
        sl = pl.ds(r0, DIL_COPY_ROWS)
        ls = [lbuf_ref[g, sl, :] for g in range(len(DIL_RATES))]
        mx = functools.reduce(jnp.maximum, ls)
        es = [jnp.exp(l - mx) for l in ls]
        num = functools.reduce(lambda a, b_: a + b_, [e * obuf_ref[g, sl, :] for g, e in enumerate(es)])
        den = functools.reduce(lambda a, b_: a + b_, es)
        o_ref[0, sl, :] = (num / den).astype(o_ref.dtype)
        return c

    lax.fori_loop(0, seq // DIL_COPY_ROWS, merge, 0)


def _dilated(proj, q_col, k_col, v_col):
    b, s, _ = proj.shape
    h = N_HEADS_HALF
    ng = len(DIL_RATES)
    full = lambda col: pl.BlockSpec((1, s, HEAD_DIM), lambda bi, hi: (bi, 0, col + hi))
    return pl.pallas_call(
        functools.partial(_dil_kernel, seq=s),
        out_shape=jax.ShapeDtypeStruct((b, s, GROUP_W), BF16),
        grid=(b, h),
        in_specs=[full(q_col), full(k_col), full(v_col)],
        out_specs=pl.BlockSpec((1, s, HEAD_DIM), lambda bi, hi: (bi, 0, hi)),
        scratch_shapes=[pltpu.VMEM((s, HEAD_DIM), BF16),
                        pltpu.VMEM((s + DIL_SPAN, HEAD_DIM), BF16),
                        pltpu.VMEM((s + DIL_SPAN, HEAD_DIM), BF16),
                        pltpu.VMEM((s, HEAD_DIM), F32), pltpu.VMEM((s, HEAD_DIM), F32),
                        pltpu.VMEM((s, HEAD_DIM), F32),
                        pltpu.VMEM((ng, s, HEAD_DIM), F32), pltpu.VMEM((ng, s, HEAD_DIM), F32)],
        compiler_params=_cparams("parallel", "parallel"),
        name="dilated_attn",
    )(proj, proj, proj)


def _residual_epilogue(y, h_ref, gpost_ref, gnext_ref, h_out_ref, u_out_ref):
    hn = h_ref[...] + _rms(y, gpost_ref[...])
    h_out_ref[...] = hn
    if u_out_ref is not None:
        u_out_ref[...] = _rms(hn, gnext_ref[...]).astype(u_out_ref.dtype)


def _outproj_kernel(a_ref, b_ref, wa_ref, wb_ref, h_ref, gpost_ref, gnext_ref, h_out_ref, u_out_ref):
    y = _dot(a_ref[...], wa_ref[...]) + _dot(b_ref[...], wb_ref[...])
    _residual_epilogue(y, h_ref, gpost_ref, gnext_ref, h_out_ref, u_out_ref)


def _outproj(a, b, wa, wb, h, g_post, g_next, tm=OUTPROJ_TM):
    m, d = h.shape
    ka = a.shape[1]
    kb = b.shape[1]
    rowblk = lambda w: pl.BlockSpec((tm, w), lambda i: (i, 0))
    const = lambda r, c: pl.BlockSpec((r, c), lambda i: (0, 0))
    return pl.pallas_call(
        _outproj_kernel,
        out_shape=(jax.ShapeDtypeStruct((m, d), F32), jax.ShapeDtypeStruct((m, d), BF16)),
        grid=(m // tm,),
        in_specs=[rowblk(ka), rowblk(kb), const(ka, d), const(kb, d), rowblk(d), const(1, d), const(1, d)],
        out_specs=(rowblk(d), rowblk(d)),
        compiler_params=_cparams("parallel"),
        name="outproj_residual",
    )(a, b, wa, wb, h, g_post.reshape(1, d), g_next.reshape(1, d))


def _ffn_kernel(u_ref, wg_ref, wu_ref, wd_ref, h_ref, gpost_ref, gnext_ref, *rest, emit_next):
    if emit_next:
        h_out_ref, u_out_ref, acc_ref = rest
    else:
        h_out_ref, acc_ref = rest
        u_out_ref = None
    f = pl.program_id(1)

    @pl.when(f == 0)
    def _():
        acc_ref[...] = jnp.zeros_like(acc_ref)

    u = u_ref[...]
    gate = _dot(u, wg_ref[...])
    up = _dot(u, wu_ref[...])
    act = (gate * _sigmoid(gate) * up).astype(BF16)
    acc_ref[...] += _dot(act, wd_ref[...])

    @pl.when(f == pl.num_programs(1) - 1)
    def _():
        _residual_epilogue(acc_ref[...], h_ref, gpost_ref, gnext_ref, h_out_ref, u_out_ref)


def _ffn(u, wg, wu, wd, h, g_post, g_next, emit_next, tm=FFN_TM, tf=FFN_TF):
    m, d = h.shape
    ff = wg.shape[1]
    rowblk = pl.BlockSpec((tm, d), lambda i, f: (i, 0))
    const = pl.BlockSpec((1, d), lambda i, f: (0, 0))
    out_shape = [jax.ShapeDtypeStruct((m, d), F32)]
    out_specs = [rowblk]
    if emit_next:
        out_shape.append(jax.ShapeDtypeStruct((m, d), BF16))
        out_specs.append(rowblk)
    res = pl.pallas_call(
        functools.partial(_ffn_kernel, emit_next=emit_next),
        out_shape=tuple(out_shape),
        grid=(m // tm, ff // tf),
        in_specs=[rowblk,
                  pl.BlockSpec((d, tf), lambda i, f: (0, f)),
                  pl.BlockSpec((d, tf), lambda i, f: (0, f)),
                  pl.BlockSpec((tf, d), lambda i, f: (f, 0)),
                  rowblk, const, const],
        out_specs=tuple(out_specs),
        scratch_shapes=[pltpu.VMEM((tm, d), F32)],
        compiler_params=_cparams("parallel", "arbitrary"),
        name="swiglu_residual",
    )(u, wg, wu, wd, h, g_post.reshape(1, d), g_next.reshape(1, d))
    return res if emit_next else (res[0], None)


def _pad_lanes(vec, offset=0):
    out = jnp.zeros((1, V7X_LANES), F32)
    return lax.dynamic_update_slice(out, vec.reshape(1, -1).astype(F32), (0, offset))


def kernel(x, mix_norm_pre, mix_norm_post, ffn_norm_pre, ffn_norm_post, ev_w_in, ev_conv_w, ev_a_log,
           ev_dt_bias, ev_onorm, ev_w_out, od_w_in, od_w_out, ffn_w_gate, ffn_w_up, ffn_w_down):
    b, s, d = x.shape
    m = b * s
    gw = GROUP_W
    nh = N_HEADS_HALF
    h = x.reshape(m, d)

    u = _prenorm(h, mix_norm_pre[0])
    w_in = ev_w_in[0]
    n_sb = 3 * gw
    n_gdn = 4 * gw
    proj_sb = _matmul(u, w_in[:, :n_sb].astype(BF16), out_dtype=BF16).reshape(b, s, n_sb)
    proj_gdn = _matmul(u, w_in[:, n_sb:n_sb + n_gdn].astype(BF16)).reshape(b, s, n_gdn)
    w_ab = jnp.pad(w_in[:, n_sb + n_gdn:], ((0, 0), (0, V7X_LANES - 2 * nh))).astype(BF16)
    proj_ab = _matmul(u, w_ab).reshape(b, s, V7X_LANES)
    o_sb = _sb_attention(proj_sb, 0, nh, 2 * nh)
    o_gdn = _gdn(proj_gdn, proj_ab, ev_conv_w[0], _pad_lanes(ev_a_log[0]), _pad_lanes(ev_dt_bias[0]),
                 ev_onorm[0].reshape(1, HEAD_DIM))
    w_out = ev_w_out[0].astype(BF16)
    h, u = _outproj(o_sb.reshape(m, gw), o_gdn.reshape(m, gw), w_out[:gw], w_out[gw:], h,
                    mix_norm_post[0], ffn_norm_pre[0])
    h, u = _ffn(u, ffn_w_gate[0].astype(BF16), ffn_w_up[0].astype(BF16), ffn_w_down[0].astype(BF16), h,
                ffn_norm_post[0], mix_norm_pre[1], emit_next=True)

    proj_od = _matmul(u, od_w_in[0].astype(BF16)).reshape(b, s, 6 * gw)
    o_moba = _moba(proj_od, 0, nh, 2 * nh)
    o_dil = _dilated(proj_od, 3 * nh, 4 * nh, 5 * nh)
    w_out = od_w_out[0].astype(BF16)
    h, u = _outproj(o_moba.reshape(m, gw), o_dil.reshape(m, gw), w_out[:gw], w_out[gw:], h,
                    mix_norm_post[1], ffn_norm_pre[1])
    h, _ = _ffn(u, ffn_w_gate[1].astype(BF16), ffn_w_up[1].astype(BF16), ffn_w_down[1].astype(BF16), h,
                ffn_norm_post[1], ffn_norm_post[1], emit_next=False)
    return h.reshape(b, s, d)
```

```python
import functools

import jax
import jax.numpy as jnp
from jax import lax
from jax.experimental import pallas as pl
from jax.experimental.pallas import tpu as pltpu

F32 = jnp.float32
BF16 = jnp.bfloat16

HEAD_DIM = 128
N_HEADS_HALF = 8
GROUP_W = N_HEADS_HALF * HEAD_DIM
NORM_EPS = 1e-6
GDN_CHUNK = 64
GDN_CONV = 4
MOBA_BLOCK = 256
MOBA_TOPK = 3
DIL_SPAN = 128
DIL_RATES = (1, 4, 16)

V7X_LANES = 128
V7X_VMEM_LIMIT_BYTES = 56 * 1024 * 1024

PRENORM_TM = 512
PROJ_TM, PROJ_TN = 2048, 512
OUTPROJ_TM = 512
FFN_TM, FFN_TF = 512, 512

NEG_INF = float("-inf")


def _cparams(*sem):
    return pltpu.CompilerParams(dimension_semantics=sem, vmem_limit_bytes=V7X_VMEM_LIMIT_BYTES)


def _dot(a, b):
    return jnp.dot(a, b, preferred_element_type=F32)


def _dot_nt(a, b):
    return lax.dot_general(a, b, (((1,), (1,)), ((), ())), preferred_element_type=F32)


def _dot_tn(a, b):
    return lax.dot_general(a, b, (((0,), (0,)), ((), ())), preferred_element_type=F32)


def _split2(x):
    hi = x.astype(BF16)
    lo = (x - hi.astype(F32)).astype(BF16)
    return hi, lo


def _split3(x):
    p1 = x.astype(BF16)
    r1 = x - p1.astype(F32)
    p2 = r1.astype(BF16)
    p3 = (r1 - p2.astype(F32)).astype(BF16)
    return p1, p2, p3


def _dot3_nt(a, b):
    a1, a2 = _split2(a)
    b1, b2 = _split2(b)
    return _dot_nt(a1, b1) + _dot_nt(a1, b2) + _dot_nt(a2, b1)


def _softplus(x):
    return jnp.maximum(x, 0.0) + jnp.log1p(jnp.exp(-jnp.abs(x)))


def _sigmoid(x):
    return 1.0 / (1.0 + jnp.exp(-x))


def _rms(x, gain):
    return x * lax.rsqrt(jnp.mean(x * x, axis=-1, keepdims=True) + NORM_EPS) * gain


def _prenorm_kernel(x_ref, g_ref, o_ref):
    o_ref[...] = _rms(x_ref[...], g_ref[...]).astype(o_ref.dtype)


def _prenorm(x2d, gain, tm=PRENORM_TM):
    m, d = x2d.shape
    return pl.pallas_call(
        _prenorm_kernel,
        out_shape=jax.ShapeDtypeStruct((m, d), BF16),
        grid=(m // tm,),
        in_specs=[pl.BlockSpec((tm, d), lambda i: (i, 0)),
                  pl.BlockSpec((1, d), lambda i: (0, 0))],
        out_specs=pl.BlockSpec((tm, d), lambda i: (i, 0)),
        compiler_params=_cparams("parallel"),
        name="prenorm",
    )(x2d, gain.reshape(1, d))


def _matmul_kernel(x_ref, w_ref, o_ref):
    o_ref[...] = _dot(x_ref[...], w_ref[...]).astype(o_ref.dtype)


def _matmul(x, w, out_dtype=F32, tm=PROJ_TM, tn=PROJ_TN):
    m, k = x.shape
    n = w.shape[1]
    tn = min(tn, n)
    return pl.pallas_call(
        _matmul_kernel,
        out_shape=jax.ShapeDtypeStruct((m, n), out_dtype),
        grid=(m // tm, n // tn),
        in_specs=[pl.BlockSpec((tm, k), lambda i, j: (i, 0)),
                  pl.BlockSpec((k, tn), lambda i, j: (0, j))],
        out_specs=pl.BlockSpec((tm, tn), lambda i, j: (i, j)),
        compiler_params=_cparams("parallel", "arbitrary"),
        name="proj_matmul",
    )(x, w)


SB_TQ = 512
SB_TK = 512
SB_PAIR = 2 * V7X_LANES


def _sb_sweep_step(q, kb_ref, vb_ref, acc_ref, carry_ref, suffix_op, start, qpos, masked):
    scale = HEAD_DIM ** -0.5
    log2e = 1.4426950408889634
    nl = V7X_LANES
    pairs = range(SB_TK // SB_PAIR - 1, -1, -1)
    p0s = [pl.multiple_of(start + pair * SB_PAIR, SB_PAIR) for pair in pairs]
    raws = [_dot_nt(q, kb_ref[pl.ds(p0, SB_PAIR), :]) for p0 in p0s]
    log_sig, pasts, parts = [], [], []
    for p0, raw in zip(p0s, raws):
        z = raw * scale
        sp = jnp.maximum(z, 0.0) + jnp.log(1.0 + jnp.exp2(jnp.abs(raw) * (-scale * log2e)))
        log_sig.append(z - sp)
        if masked:
            past = (p0 + lax.broadcasted_iota(jnp.int32, (SB_TQ, SB_PAIR), 1)) < qpos
            sp = jnp.where(past, sp, 0.0)
            pasts.append(past)
        hi, lo = _split2(sp)
        parts.append((jnp.concatenate([hi[:, nl:], lo[:, nl:]], axis=1),
                      jnp.concatenate([hi[:, :nl], lo[:, :nl]], axis=1)))
    sums = [(_dot(near, suffix_op), _dot(far, suffix_op)) for near, far in parts]
    carry = carry_ref[...]
    betweens = []
    for sr_near, sr_far in sums:
        c_far = carry + sr_near[:, nl:]
        betweens.append(jnp.concatenate([sr_far[:, :nl] + c_far, sr_near[:, :nl] + carry], axis=1))
        carry = c_far + sr_far[:, nl:]
    carry_ref[...] = carry
    ws = []
    for idx, (ls, between) in enumerate(zip(log_sig, betweens)):
        w = jnp.exp2((ls - between) * log2e)
        if masked:
            w = jnp.where(pasts[idx], w, 0.0)
        ws.append(w.astype(BF16))
    acc_ref[...] += sum(_dot(w, vb_ref[pl.ds(p0, SB_PAIR), :]) for w, p0 in zip(ws, p0s))


def _sb_kernel(q_ref, k_ref, v_ref, o_ref, kb_ref, vb_ref, acc_ref, carry_ref):
    i = pl.program_id(2)

    @pl.when(i == 0)
    def _():
        kb_ref[...] = k_ref[0].astype(BF16)
        vb_ref[...] = v_ref[0].astype(BF16)

    q = q_ref[0].astype(BF16)
    acc_ref[...] = jnp.zeros_like(acc_ref)
    carry_ref[...] = jnp.zeros_like(carry_ref)

    kk = lax.broadcasted_iota(jnp.int32, (SB_PAIR, SB_PAIR), 0) % V7X_LANES
    cc = lax.broadcasted_iota(jnp.int32, (SB_PAIR, SB_PAIR), 1)
    suffix_op = jnp.where((cc >= V7X_LANES) | (kk > cc), 1.0, 0.0).astype(BF16)

    qpos = i * SB_TQ + lax.broadcasted_iota(jnp.int32, (SB_TQ, SB_PAIR), 0)
    diag = (i * SB_TQ) // SB_TK
    _sb_sweep_step(q, kb_ref, vb_ref, acc_ref, carry_ref, suffix_op, diag * SB_TK, qpos, masked=True)

    def body(t, c):
        _sb_sweep_step(q, kb_ref, vb_ref, acc_ref, carry_ref, suffix_op, (diag - t) * SB_TK, None, masked=False)
        return c

    lax.fori_loop(1, diag + 1, body, 0)
    o_ref[0] = acc_ref[...].astype(o_ref.dtype)


def _sb_attention(proj, q_col, k_col, v_col):
    b, s, _ = proj.shape
    h = N_HEADS_HALF
    return pl.pallas_call(
        _sb_kernel,
        out_shape=jax.ShapeDtypeStruct((b, s, GROUP_W), BF16),
        grid=(b, h, s // SB_TQ),
        in_specs=[pl.BlockSpec((1, SB_TQ, HEAD_DIM), lambda bi, hi, i: (bi, i, q_col + hi)),
                  pl.BlockSpec((1, s, HEAD_DIM), lambda bi, hi, i: (bi, 0, k_col + hi)),
                  pl.BlockSpec((1, s, HEAD_DIM), lambda bi, hi, i: (bi, 0, v_col + hi))],
        out_specs=pl.BlockSpec((1, SB_TQ, HEAD_DIM), lambda bi, hi, i: (bi, i, hi)),
        scratch_shapes=[pltpu.VMEM((s, HEAD_DIM), BF16), pltpu.VMEM((s, HEAD_DIM), BF16),
                        pltpu.VMEM((SB_TQ, HEAD_DIM), F32), pltpu.VMEM((SB_TQ, HEAD_DIM), F32)],
        compiler_params=_cparams("parallel", "parallel", "arbitrary"),
        name="stickbreak_attn",
    )(proj, proj, proj)


GDN_ROWS = 512
GDN_GROUP = 256
GDN_HALO = 8
GDN_HEADS_PER_STEP = 8


def _gdn_prep(hd, xq_ref, xk_ref, xv_ref, ab_ref, wq_ref, wk_ref, wv_ref, alog_ref, dtb_ref, xbuf_ref):
    h = pl.program_id(1) * GDN_HEADS_PER_STEP + hd
    hl = slice(hd * HEAD_DIM, (hd + 1) * HEAD_DIM)
    rows = GDN_ROWS
    c = GDN_CHUNK

    conv = []
    for idx, (x_ref, cw_ref) in enumerate(((xq_ref, wq_ref), (xk_ref, wk_ref), (xv_ref, wv_ref))):
        xbuf_ref[hd, idx, GDN_HALO:GDN_HALO + rows, :] = x_ref[0, :, hl]
        y = jnp.zeros((rows, HEAD_DIM), F32)
        for tap in range(GDN_CONV):
            shift = GDN_CONV - 1 - tap
            y = y + cw_ref[tap:tap + 1, hl] * xbuf_ref[hd, idx, GDN_HALO - shift:GDN_HALO - shift + rows, :]
        xbuf_ref[hd, idx, 0:GDN_HALO, :] = xbuf_ref[hd, idx, rows:rows + GDN_HALO, :]
        conv.append(y * _sigmoid(y))
    yq, yk, yv = conv
    q = yq * lax.rsqrt(jnp.sum(yq * yq, axis=-1, keepdims=True) + NORM_EPS) * (HEAD_DIM ** -0.5)
    k = yk * lax.rsqrt(jnp.sum(yk * yk, axis=-1, keepdims=True) + NORM_EPS)
    v = yv

    ab = ab_ref[0]
    g_all = -jnp.exp(alog_ref[...]) * _softplus(ab + dtb_ref[...])
    beta_all = _sigmoid(ab)
    ab_lane = lax.broadcasted_iota(jnp.int32, (rows, V7X_LANES), 1)
    g_col = jnp.sum(jnp.where(ab_lane == h, g_all, 0.0), axis=-1, keepdims=True)
    beta = jnp.sum(jnp.where(ab_lane == h + N_HEADS_HALF, beta_all, 0.0), axis=-1, keepdims=True)
    g_rep = jnp.broadcast_to(g_col, (rows, HEAD_DIM))
    ri = lax.broadcasted_iota(jnp.int32, (GDN_GROUP, GDN_GROUP), 0)
    ci = lax.broadcasted_iota(jnp.int32, (GDN_GROUP, GDN_GROUP), 1)
    cum_op = jnp.where((ri // c == ci // c) & (ci <= ri), 1.0, 0.0).astype(BF16)
    gc = jnp.concatenate(
        [sum(_dot(cum_op, piece) for piece in _split3(g_rep[r0:r0 + GDN_GROUP]))
         for r0 in range(0, rows, GDN_GROUP)], axis=0)

    gc3 = gc.reshape(rows // c, c, HEAD_DIM)
    gl = jnp.broadcast_to(gc3[:, c - 1:c, :], (rows // c, c, HEAD_DIM)).reshape(rows, HEAD_DIM)
    eg = jnp.exp(gc)
    egl = jnp.exp(gl)
    kd = k * jnp.exp(gl - gc)
    qg = q * eg
    kb = k * beta
    rhs_uw = jnp.concatenate([v * beta, kb * eg], axis=1)
    return dict(q=q, k=k, kb=kb, kd=kd, qg=qg, gc=gc, egl=egl, rhs_uw=rhs_uw)


def _gdn_kernel(xq_ref, xk_ref, xv_ref, z_ref, ab_ref, wq_ref, wk_ref, wv_ref, alog_ref, dtb_ref,
                onorm_ref, o_ref, xbuf_ref, state_ref, u_ref, w_ref, vnew_ref, ointer_ref):
    rows = GDN_ROWS
    c = GDN_CHUNK
    nch = rows // c
    heads = range(GDN_HEADS_PER_STEP)
    groups = [(hd, slice(r0, r0 + GDN_GROUP)) for hd in heads for r0 in range(0, rows, GDN_GROUP)]

    @pl.when(pl.program_id(2) == 0)
    def _():
        state_ref[...] = jnp.zeros_like(state_ref)
        xbuf_ref[:, :, 0:GDN_HALO, :] = jnp.zeros((GDN_HEADS_PER_STEP, 3, GDN_HALO, HEAD_DIM), F32)

    hv = [_gdn_prep(hd, xq_ref, xk_ref, xv_ref, ab_ref, wq_ref, wk_ref, wv_ref, alog_ref, dtb_ref, xbuf_ref)
          for hd in heads]

    lane = lax.broadcasted_iota(jnp.int32, (GDN_GROUP, V7X_LANES), 1)
    gi = lax.broadcasted_iota(jnp.int32, (GDN_GROUP, GDN_GROUP), 0)
    gj = lax.broadcasted_iota(jnp.int32, (GDN_GROUP, GDN_GROUP), 1)
    same = (gi // c) == (gj // c)
    lower_incl = same & (gj <= gi)
    lower_strict = same & (gj < gi)
    eye = jnp.where(gi == gj, 1.0, 0.0)

    ps, xs, intras = [], [], []
    for hd, sl in groups:
        p1, p2, p3 = (piece.astype(F32) for piece in _split3(hv[hd]["gc"][sl]))
        lhs = jnp.where(lane == 0, p1, jnp.where(lane == 1, p2, jnp.where(lane == 2, p3,
              jnp.where(lane < 6, 1.0, 0.0))))
        rhs = jnp.where(lane < 3, 1.0, jnp.where(lane == 3, -p1, jnp.where(lane == 4, -p2,
              jnp.where(lane == 5, -p3, 0.0))))
        diff = _dot_nt(lhs.astype(BF16), rhs.astype(BF16))
        decay = jnp.exp(jnp.where(lower_incl, diff, 0.0))
        kgb = hv[hd]["k"][sl].astype(BF16)
        a_mat = jnp.where(lower_strict, _dot_nt(hv[hd]["kb"][sl].astype(BF16), kgb) * decay, 0.0)
        intras.append(jnp.where(lower_incl, _dot_nt(hv[hd]["q"][sl].astype(BF16), kgb) * decay, 0.0).astype(BF16))
        ps.append(-a_mat)
        xs.append(eye - a_mat)

    for _ in range(5):
        pbs = [p.astype(BF16) for p in ps]
        ps = [_dot(pb, pb) for pb in pbs]
        xs = [x + _dot(x.astype(BF16), p.astype(BF16)) for x, p in zip(xs, ps)]
    for (hd, sl), x in zip(groups, xs):
        uw = _dot(x.astype(BF16), hv[hd]["rhs_uw"][sl].astype(BF16))
        u_ref[hd, sl, :] = uw[:, :HEAD_DIM]
        w_ref[hd, sl, :] = uw[:, HEAD_DIM:]

    chunk = lambda ch: slice(ch * c, (ch + 1) * c)
    trans = [[None] * nch for _ in heads]
    drive = [[None] * nch for _ in heads]
    for ch in range(nch):
        for hd in heads:
            kdb = hv[hd]["kd"][chunk(ch)].astype(BF16)
            trans[hd][ch] = _dot_tn(kdb, w_ref[hd, chunk(ch), :].astype(BF16)).astype(BF16)
            drive[hd][ch] = _dot_tn(kdb, u_ref[hd, chunk(ch), :].astype(BF16))
    st = [state_ref[hd] for hd in heads]
    states = [[None] * nch for _ in heads]
    for ch in range(nch):
        for hd in heads:
            stb = st[hd].astype(BF16)
            states[hd][ch] = stb
            st[hd] = st[hd] * hv[hd]["egl"][ch * c:ch * c + 1, :] - _dot(trans[hd][ch], stb) + drive[hd][ch]
    for hd in heads:
        state_ref[hd] = st[hd]
    for ch in range(nch):
        for hd in heads:
            lhs = jnp.concatenate([w_ref[hd, chunk(ch), :], hv[hd]["qg"][chunk(ch)]], axis=0).astype(BF16)
            ws = _dot(lhs, states[hd][ch])
            vnew_ref[hd, chunk(ch), :] = u_ref[hd, chunk(ch), :] - ws[:c]
            ointer_ref[hd, chunk(ch), :] = ws[c:]

    for (hd, sl), intra in zip(groups, intras):
        hl = slice(hd * HEAD_DIM, (hd + 1) * HEAD_DIM)
        zz = z_ref[0, sl, hl]
        o = ointer_ref[hd, sl, :] + _dot(intra, vnew_ref[hd, sl, :].astype(BF16))
        o_ref[0, sl, hl] = (_rms(o, onorm_ref[...]) * (zz * _sigmoid(zz))).astype(o_ref.dtype)


def _gdn(proj, ab, conv_w, alog_row, dtb_row, onorm_row):
    b, s, _ = proj.shape
    nhd = GDN_HEADS_PER_STEP
    wblk = nhd * HEAD_DIM
    hsteps = N_HEADS_HALF // nhd
    row = lambda col: pl.BlockSpec((1, GDN_ROWS, wblk), lambda bi, hi, si: (bi, si, col + hi))
    cw = lambda col: pl.BlockSpec((GDN_CONV, wblk), lambda bi, hi, si: (0, col + hi))
    vec = pl.BlockSpec((1, V7X_LANES), lambda bi, hi, si: (0, 0))
    per_head = lambda *shape: pltpu.VMEM((nhd,) + shape, F32)
    return pl.pallas_call(
        _gdn_kernel,
        out_shape=jax.ShapeDtypeStruct((b, s, GROUP_W), BF16),
        grid=(b, hsteps, s // GDN_ROWS),
        in_specs=[row(0), row(hsteps), row(2 * hsteps), row(3 * hsteps),
                  pl.BlockSpec((1, GDN_ROWS, V7X_LANES), lambda bi, hi, si: (bi, si, 0)),
                  cw(0), cw(hsteps), cw(2 * hsteps), vec, vec, vec],
        out_specs=pl.BlockSpec((1, GDN_ROWS, wblk), lambda bi, hi, si: (bi, si, hi)),
        scratch_shapes=[per_head(3, GDN_ROWS + GDN_HALO, HEAD_DIM),
                        per_head(HEAD_DIM, HEAD_DIM),
                        per_head(GDN_ROWS, HEAD_DIM), per_head(GDN_ROWS, HEAD_DIM),
                        per_head(GDN_ROWS, HEAD_DIM), per_head(GDN_ROWS, HEAD_DIM)],
        compiler_params=_cparams("parallel", "parallel", "arbitrary"),
        name="gated_delta",
    )(proj, proj, proj, proj, ab, conv_w, conv_w, conv_w, alog_row, dtb_row, onorm_row)


MOBA_MASK_BIG = 1e30
MOBA_TQ = 4 * MOBA_BLOCK
MOBA_SWEEP_BLOCKS = (8, 4)


def _moba_kernel(q_ref, k_ref, v_ref, o_ref, kaug_ref, vb_ref, qaug_ref, m_ref, l_ref, acc_ref, sbuf_ref,
                 *, seq):
    bs = MOBA_BLOCK
    tq = MOBA_TQ
    nb = seq // bs
    nl = V7X_LANES
    ncand = -(-nb // 8) * 8
    i = (tq // bs) * pl.program_id(2)
    exp2_scale = (HEAD_DIM ** -0.5) * 1.4426950408889634

    @pl.when(i == 0)
    def _():
        kf = k_ref[0]
        kaug_ref[:, :HEAD_DIM] = kf.astype(BF16)
        blk = lax.broadcasted_iota(jnp.int32, (seq, nl), 0) // bs
        col = lax.broadcasted_iota(jnp.int32, (seq, nl), 1)
        kaug_ref[:, HEAD_DIM:] = jnp.where(blk == col, MOBA_MASK_BIG, 0.0).astype(BF16)
        vb_ref[...] = v_ref[0].astype(BF16)
        rr = lax.broadcasted_iota(jnp.int32, (V7X_LANES, seq), 0)
        ss = lax.broadcasted_iota(jnp.int32, (V7X_LANES, seq), 1)
        onehot = jnp.where(ss // bs == rr, 1.0, 0.0).astype(BF16)
        tot = jnp.zeros((V7X_LANES, HEAD_DIM), F32)
        for piece in _split3(kf):
            tot = tot + _dot(onehot, piece)
        kmean = tot * (1.0 / bs)

        qaug_ref[:, :HEAD_DIM] = q_ref[0].astype(BF16)
        cand_idx = lax.broadcasted_iota(jnp.int32, (ncand, tq), 0)
        cand_f = cand_idx.astype(F32)
        for tile in range(seq // tq):
            rows = slice(tile * tq, (tile + 1) * tq)
            own = tile * (tq // bs) + lax.broadcasted_iota(jnp.int32, (ncand, tq), 1) // bs
            gate = jnp.where(cand_idx < own, _dot3_nt(kmean, q_ref[0, rows, :])[:ncand], NEG_INF)
            picked_t = jnp.zeros((ncand, tq), F32)
            for _ in range(MOBA_TOPK):
                mx = jnp.max(gate, axis=0, keepdims=True)
                first = jnp.min(jnp.where((gate == mx) & (mx > NEG_INF), cand_f, float(ncand)),
                                axis=0, keepdims=True)
                hit = cand_f == first
                picked_t = jnp.where(hit, 1.0, picked_t)
                gate = jnp.where(hit, NEG_INF, gate)
            allowed_t = jnp.where(cand_idx == own, 1.0, picked_t)
            allowed = jnp.concatenate([allowed_t, jnp.zeros((nl - ncand, tq), F32)], axis=0).T
            qaug_ref[rows, HEAD_DIM:] = (allowed - 1.0).astype(BF16)

    q_aug = qaug_ref[pl.ds(pl.multiple_of(i * bs, tq), tq), :]

    def raw_scores(first, nblocks):
        start = pl.multiple_of(first * bs, bs)
        return _dot_nt(q_aug, kaug_ref[pl.ds(start, nblocks * bs), :])

    def lane_groups(x):
        return [x[:, g * nl:(g + 1) * nl] for g in range(x.shape[1] // nl)]

    def sweep_past(step):
        wide = MOBA_SWEEP_BLOCKS[0]

        def wide_body(t, c):
            step(t * wide, wide)
            return c

        lax.fori_loop(0, i // wide, wide_body, 0)
        for w in MOBA_SWEEP_BLOCKS[1:]:
            @pl.when((i // w) % 2 == 1)
            def _(w=w):
                step((i // (2 * w)) * (2 * w), w)

    own0 = pl.multiple_of(i * bs, tq)
    row2 = lax.broadcasted_iota(jnp.int32, (tq, tq), 0)
    col2 = lax.broadcasted_iota(jnp.int32, (tq, tq), 1)
    s_own = jnp.where(col2 <= row2, _dot_nt(q_aug, kaug_ref[pl.ds(own0, tq), :]), -MOBA_MASK_BIG)

    m_ref[...] = functools.reduce(jnp.maximum, lane_groups(s_own))

    def max_step(first, nblocks):
        sc = raw_scores(first, nblocks)
        sbuf_ref[:, pl.ds(pl.multiple_of(first * bs, bs), nblocks * bs)] = sc
        m_ref[...] = functools.reduce(jnp.maximum, [m_ref[...]] + lane_groups(sc))

    sweep_past(max_step)
    m_ref[...] = jnp.broadcast_to(jnp.max(m_ref[...], axis=-1, keepdims=True), (tq, nl))

    def accumulate(sc, first_row, init):
        mrow = m_ref[...]
        ps = [jnp.exp2((g - mrow) * exp2_scale) for g in lane_groups(sc)]
        psum = functools.reduce(lambda a, b_: a + b_, ps)
        pv = _dot(jnp.concatenate(ps, axis=1).astype(BF16), vb_ref[pl.ds(first_row, sc.shape[1]), :])
        if init:
            l_ref[...] = psum
            acc_ref[...] = pv
        else:
            l_ref[...] += psum
            acc_ref[...] += pv

    accumulate(s_own, own0, init=True)

    def acc_step(first, nblocks):
        start = pl.multiple_of(first * bs, bs)
        accumulate(sbuf_ref[:, pl.ds(start, nblocks * bs)], start, init=False)

    sweep_past(acc_step)
    o_ref[0] = (acc_ref[...] / jnp.sum(l_ref[...], axis=-1, keepdims=True)).astype(o_ref.dtype)


def _moba(proj, q_col, k_col, v_col):
    b, s, _ = proj.shape
    h = N_HEADS_HALF
    tq = MOBA_TQ
    assert s % (MOBA_BLOCK * MOBA_SWEEP_BLOCKS[0]) == 0 and s // MOBA_BLOCK <= V7X_LANES
    return pl.pallas_call(
        functools.partial(_moba_kernel, seq=s),
        out_shape=jax.ShapeDtypeStruct((b, s, GROUP_W), BF16),
        grid=(b, h, s // tq),
        in_specs=[pl.BlockSpec((1, s, HEAD_DIM), lambda bi, hi, i: (bi, 0, q_col + hi)),
                  pl.BlockSpec((1, s, HEAD_DIM), lambda bi, hi, i: (bi, 0, k_col + hi)),
                  pl.BlockSpec((1, s, HEAD_DIM), lambda bi, hi, i: (bi, 0, v_col + hi))],
        out_specs=pl.BlockSpec((1, tq, HEAD_DIM), lambda bi, hi, i: (bi, i, hi)),
        scratch_shapes=[pltpu.VMEM((s, HEAD_DIM + V7X_LANES), BF16), pltpu.VMEM((s, HEAD_DIM), BF16),
                        pltpu.VMEM((s, HEAD_DIM + V7X_LANES), BF16),
                        pltpu.VMEM((tq, V7X_LANES), F32), pltpu.VMEM((tq, V7X_LANES), F32),
                        pltpu.VMEM((tq, HEAD_DIM), F32), pltpu.VMEM((tq, s), F32)],
        compiler_params=_cparams("parallel", "parallel", "arbitrary"),
        name="moba_attn",
    )(proj, proj, proj)


DIL_COPY_ROWS = 256
DIL_BLOCKS_PER_STEP = 8


def _dil_kernel(q_ref, k_ref, v_ref, o_ref, qd_ref, kd_ref, vd_ref, tq_ref, tk_ref, tv_ref, obuf_ref, lbuf_ref,
                *, seq):
    span = DIL_SPAN
    scale = HEAD_DIM ** -0.5
    kd_ref[0:span, :] = jnp.zeros((span, HEAD_DIM), BF16)
    vd_ref[0:span, :] = jnp.zeros((span, HEAD_DIM), BF16)
    qi = lax.broadcasted_iota(jnp.int32, (span, 2 * span), 0)
    ki = lax.broadcasted_iota(jnp.int32, (span, 2 * span), 1)
    in_window = (ki >= qi) & (ki <= qi + span)

    for g, rate in enumerate(DIL_RATES):
        n = seq // rate
        copies_per_res = n // DIL_COPY_ROWS
        nblk = n // span

        prev = DIL_RATES[g - 1] if g > 0 else 1
        staged = prev > 1 and rate % prev == 0
        keep_f32 = rate > 1 and g + 1 < len(DIL_RATES) and DIL_RATES[g + 1] % rate == 0
        step_rows = rate // prev if staged else rate

        def copy(c, cc, rate=rate, copies_per_res=copies_per_res, prev=prev, staged=staged,
                 keep_f32=keep_f32, step_rows=step_rows):
            rho = c // copies_per_res
            within = (c % copies_per_res) * (DIL_COPY_ROWS * step_rows)
            if staged:
                src = (rho % prev) * (seq // prev) + rho // prev + within
            else:
                src = rho + within
            dst = pl.multiple_of(c * DIL_COPY_ROWS, DIL_COPY_ROWS)
            idx = pl.ds(src, DIL_COPY_ROWS) if step_rows == 1 else pl.ds(src, DIL_COPY_ROWS, stride=step_rows)
            for x_ref, t_ref, d_ref, off in ((q_ref, tq_ref, qd_ref, 0), (k_ref, tk_ref, kd_ref, span),
                                             (v_ref, tv_ref, vd_ref, span)):
                x = t_ref[idx, :] if staged else x_ref[0, idx, :]
                if keep_f32:
                    t_ref[pl.ds(dst, DIL_COPY_ROWS), :] = x
                d_ref[pl.ds(off + dst, DIL_COPY_ROWS), :] = x.astype(BF16)
            return cc

        lax.fori_loop(0, seq // DIL_COPY_ROWS, copy, 0)

        def blocks(step, cc, g=g, rate=rate, nblk=nblk):
            blks = [step * DIL_BLOCKS_PER_STEP + sub for sub in range(DIL_BLOCKS_PER_STEP)]
            r0s = [pl.multiple_of(blk * span, span) for blk in blks]
            raw = [_dot_nt(qd_ref[pl.ds(r0, span), :], kd_ref[pl.ds(r0, 2 * span), :]) for r0 in r0s]
            scs = [jnp.where(in_window & ((blk % nblk > 0) | (ki >= span)), s * scale, NEG_INF)
                   for blk, s in zip(blks, raw)]
            ms = [jnp.max(sc, axis=-1, keepdims=True) for sc in scs]
            pr = [jnp.exp(sc - m) for sc, m in zip(scs, ms)]
            dens = [jnp.sum(p, axis=-1, keepdims=True) for p in pr]
            pvs = [_dot(p.astype(BF16), vd_ref[pl.ds(r0, 2 * span), :]) for p, r0 in zip(pr, r0s)]
            for blk, pv, m, den in zip(blks, pvs, ms, dens):
                dst = blk // nblk + (blk % nblk) * (span * rate)
                idx = pl.ds(dst, span) if rate == 1 else pl.ds(dst, span, stride=rate)
                obuf_ref[g, idx, :] = pv / den
                lbuf_ref[g, idx, :] = jnp.broadcast_to(m + jnp.log(den), (span, HEAD_DIM))
            return cc

        lax.fori_loop(0, seq // span // DIL_BLOCKS_PER_STEP, blocks, 0)

    def merge(t, c):
        r0 = pl.multiple_of(t * DIL_COPY_ROWS, DIL_COPY_ROWS)
        sl = pl.ds(r0, DIL_COPY_ROWS)
        ls = [lbuf_ref[g, sl, :] for g in range(len(DIL_RATES))]
        mx = functools.reduce(jnp.maximum, ls)
        es = [jnp.exp(l - mx) for l in ls]
        num = functools.reduce(lambda a, b_: a + b_, [e * obuf_ref[g, sl, :] for g, e in enumerate(es)])
        den = functools.reduce(lambda a, b_: a + b_, es)
        o_ref[0, sl, :] = (num / den).astype(o_ref.dtype)
        return c

    lax.fori_loop(0, seq // DIL_COPY_ROWS, merge, 0)


def _dilated(proj, q_col, k_col, v_col):
    b, s, _ = proj.shape
    h = N_HEADS_HALF
    ng = len(DIL_RATES)
    full = lambda col: pl.BlockSpec((1, s, HEAD_DIM), lambda bi, hi: (bi, 0, col + hi))
    return pl.pallas_call(
        functools.partial(_dil_kernel, seq=s),
        out_shape=jax.ShapeDtypeStruct((b, s, GROUP_W), BF16),
        grid=(b, h),
        in_specs=[full(q_col), full(k_col), full(v_col)],
        out_specs=pl.BlockSpec((1, s, HEAD_DIM), lambda bi, hi: (bi, 0, hi)),
        scratch_shapes=[pltpu.VMEM((s, HEAD_DIM), BF16),
                        pltpu.VMEM((s + DIL_SPAN, HEAD_DIM), BF16),
                        pltpu.VMEM((s + DIL_SPAN, HEAD_DIM), BF16),
                        pltpu.VMEM((s, HEAD_DIM), F32), pltpu.VMEM((s, HEAD_DIM), F32),
                        pltpu.VMEM((s, HEAD_DIM), F32),
                        pltpu.VMEM((ng, s, HEAD_DIM), F32), pltpu.VMEM((ng, s, HEAD_DIM), F32)],
        compiler_params=_cparams("parallel", "parallel"),
        name="dilated_attn",
    )(proj, proj, proj)


def _residual_epilogue(y, h_ref, gpost_ref, gnext_ref, h_out_ref, u_out_ref):
    hn = h_ref[...] + _rms(y, gpost_ref[...])
    h_out_ref[...] = hn
    if u_out_ref is not None:
        u_out_ref[...] = _rms(hn, gnext_ref[...]).astype(u_out_ref.dtype)


def _outproj_kernel(a_ref, b_ref, wa_ref, wb_ref, h_ref, gpost_ref, gnext_ref, h_out_ref, u_out_ref):
    y = _dot(a_ref[...], wa_ref[...]) + _dot(b_ref[...], wb_ref[...])
    _residual_epilogue(y, h_ref, gpost_ref, gnext_ref, h_out_ref, u_out_ref)


def _outproj(a, b, wa, wb, h, g_post, g_next, tm=OUTPROJ_TM):
    m, d = h.shape
    ka = a.shape[1]
    kb = b.shape[1]
    rowblk = lambda w: pl.BlockSpec((tm, w), lambda i: (i, 0))
    const = lambda r, c: pl.BlockSpec((r, c), lambda i: (0, 0))
    return pl.pallas_call(
        _outproj_kernel,
        out_shape=(jax.ShapeDtypeStruct((m, d), F32), jax.ShapeDtypeStruct((m, d), BF16)),
        grid=(m // tm,),
        in_specs=[rowblk(ka), rowblk(kb), const(ka, d), const(kb, d), rowblk(d), const(1, d), const(1, d)],
        out_specs=(rowblk(d), rowblk(d)),
        compiler_params=_cparams("parallel"),
        name="outproj_residual",
    )(a, b, wa, wb, h, g_post.reshape(1, d), g_next.reshape(1, d))


def _ffn_kernel(u_ref, wg_ref, wu_ref, wd_ref, h_ref, gpost_ref, gnext_ref, *rest, emit_next):
    if emit_next:
        h_out_ref, u_out_ref, acc_ref = rest
    else:
        h_out_ref, acc_ref = rest
        u_out_ref = None
    f = pl.program_id(1)

    @pl.when(f == 0)
    def _():
        acc_ref[...] = jnp.zeros_like(acc_ref)

    u = u_ref[...]
    gate = _dot(u, wg_ref[...])
    up = _dot(u, wu_ref[...])
    act = (gate * _sigmoid(gate) * up).astype(BF16)
    acc_ref[...] += _dot(act, wd_ref[...])

    @pl.when(f == pl.num_programs(1) - 1)
    def _():
        _residual_epilogue(acc_ref[...], h_ref, gpost_ref, gnext_ref, h_out_ref, u_out_ref)


def _ffn(u, wg, wu, wd, h, g_post, g_next, emit_next, tm=FFN_TM, tf=FFN_TF):
    m, d = h.shape
    ff = wg.shape[1]
    rowblk = pl.BlockSpec((tm, d), lambda i, f: (i, 0))
    const = pl.BlockSpec((1, d), lambda i, f: (0, 0))
    out_shape = [jax.ShapeDtypeStruct((m, d), F32)]
    out_specs = [rowblk]
    if emit_next:
        out_shape.append(jax.ShapeDtypeStruct((m, d), BF16))
        out_specs.append(rowblk)
    res = pl.pallas_call(
        functools.partial(_ffn_kernel, emit_next=emit_next),
        out_shape=tuple(out_shape),
        grid=(m // tm, ff // tf),
        in_specs=[rowblk,
                  pl.BlockSpec((d, tf), lambda i, f: (0, f)),
                  pl.BlockSpec((d, tf), lambda i, f: (0, f)),
                  pl.BlockSpec((tf, d), lambda i, f: (f, 0)),
                  rowblk, const, const],
        out_specs=tuple(out_specs),
        scratch_shapes=[pltpu.VMEM((tm, d), F32)],
        compiler_params=_cparams("parallel", "arbitrary"),
        name="swiglu_residual",
    )(u, wg, wu, wd, h, g_post.reshape(1, d), g_next.reshape(1, d))
    return res if emit_next else (res[0], None)


def _pad_lanes(vec, offset=0):
    out = jnp.zeros((1, V7X_LANES), F32)
    return lax.dynamic_update_slice(out, vec.reshape(1, -1).astype(F32), (0, offset))


def kernel(x, mix_norm_pre, mix_norm_post, ffn_norm_pre, ffn_norm_post, ev_w_in, ev_conv_w, ev_a_log,
           ev_dt_bias, ev_onorm, ev_w_out, od_w_in, od_w_out, ffn_w_gate, ffn_w_up, ffn_w_down):
    b, s, d = x.shape
    m = b * s
    gw = GROUP_W
    nh = N_HEADS_HALF
    h = x.reshape(m, d)

    u = _prenorm(h, mix_norm_pre[0])
    w_in = ev_w_in[0]
    n_sb = 3 * gw
    n_gdn = 4 * gw
    proj_sb = _matmul(u, w_in[:, :n_sb].astype(BF16), out_dtype=BF16).reshape(b, s, n_sb)
    proj_gdn = _matmul(u, w_in[:, n_sb:n_sb + n_gdn].astype(BF16)).reshape(b, s, n_gdn)
    w_ab = jnp.pad(w_in[:, n_sb + n_gdn:], ((0, 0), (0, V7X_LANES - 2 * nh))).astype(BF16)
    proj_ab = _matmul(u, w_ab).reshape(b, s, V7X_LANES)
    o_sb = _sb_attention(proj_sb, 0, nh, 2 * nh)
    o_gdn = _gdn(proj_gdn, proj_ab, ev_conv_w[0], _pad_lanes(ev_a_log[0]), _pad_lanes(ev_dt_bias[0]),
                 ev_onorm[0].reshape(1, HEAD_DIM))
    w_out = ev_w_out[0].astype(BF16)
    h, u = _outproj(o_sb.reshape(m, gw), o_gdn.reshape(m, gw), w_out[:gw], w_out[gw:], h,
                    mix_norm_post[0], ffn_norm_pre[0])
    h, u = _ffn(u, ffn_w_gate[0].astype(BF16), ffn_w_up[0].astype(BF16), ffn_w_down[0].astype(BF16), h,
                ffn_norm_post[0], mix_norm_pre[1], emit_next=True)

    proj_od = _matmul(u, od_w_in[0].astype(BF16)).reshape(b, s, 6 * gw)
    o_moba = _moba(proj_od, 0, nh, 2 * nh)
    o_dil = _dilated(proj_od, 3 * nh, 4 * nh, 5 * nh)
    w_out = od_w_out[0].astype(BF16)
    h, u = _outproj(o_moba.reshape(m, gw), o_dil.reshape(m, gw), w_out[:gw], w_out[gw:], h,
                    mix_norm_post[1], ffn_norm_pre[1])
    h, _ = _ffn(u, ffn_w_gate[1].astype(BF16), ffn_w_up[1].astype(BF16), ffn_w_down[1].astype(BF16), h,
                ffn_norm_post[1], ffn_norm_post[1], emit_next=False)
    return h.reshape(b, s, d)
```
